```python
import math, functools
import jax, jax.numpy as jnp
from jax import lax
import numpy as np

D_MODEL = 2048
BATCH = 1
SEQ = 8192
DEPTH = 1
DEC_BATCH = 128
DEC_SEQ = 1
PAST_LEN = 16384
PAGE_SIZE = 128

D_CONV = D_MODEL // 2
CONV_GROUPS = 8
CONV_W = 3
N_HEADS = 8
QK_NOPE = 128
QK_ROPE = 64
V_HEAD = 128
Q_LORA = 512
KV_LORA = 512
ROPE_THETA = 10000.0
ATTN_SCALE = (QK_NOPE + QK_ROPE) ** -0.5
Q_BLOCK = 128
D_MIX = D_CONV + N_HEADS * V_HEAD
IN_SPLITS = [D_CONV, 2 * D_CONV, 3 * D_CONV, 3 * D_CONV + Q_LORA, 3 * D_CONV + Q_LORA + KV_LORA]
D_IN = 3 * D_CONV + Q_LORA + KV_LORA + QK_ROPE
N_GROUPS = 4
EXPERTS_PER_GROUP = 8
N_EXPERTS = N_GROUPS * EXPERTS_PER_GROUP
TOP_K = 2
D_EXPERT = 512
ROUTE_BLOCK = 128
EPS = 1e-6

kernel_name = 'hymba_conv_mla_hiermoe_adaln_step'


def rmsnorm(x, g):
    xf = x.astype(jnp.float32)
    y = xf * lax.rsqrt(jnp.mean(xf * xf, axis=-1, keepdims=True) + EPS)
    return (y * g.astype(jnp.float32)).astype(x.dtype)


def ada_modulation(c, w_ada, b_ada):
    mod = jax.nn.silu(c) @ w_ada + b_ada
    return jnp.split(mod[:, None, :], 6, axis=-1)


def rope(x, pos):
    half = QK_ROPE // 2
    inv_freq = ROPE_THETA ** (-jnp.arange(half, dtype=jnp.float32) / half)
    ang = pos.astype(jnp.float32)[:, None, None] * inv_freq
    cos, sin = jnp.cos(ang), jnp.sin(ang)
    x1 = x[..., :half].astype(jnp.float32)
    x2 = x[..., half:].astype(jnp.float32)
    out = jnp.concatenate([x1 * cos - x2 * sin, x2 * cos + x1 * sin], axis=-1)
    return out.astype(x.dtype)


def mla_prompt(q_nope, q_rope, ckv, kr, w_ukv):
    B, T, H, _ = q_nope.shape
    kv = (ckv @ w_ukv).reshape(B, T, H, QK_NOPE + V_HEAD)
    k = jnp.concatenate([kv[..., :QK_NOPE], jnp.broadcast_to(kr[:, :, None, :], (B, T, H, QK_ROPE))], axis=-1)
    v = kv[..., QK_NOPE:]
    q = jnp.concatenate([q_nope, q_rope], axis=-1)
    nb = T // Q_BLOCK
    qb = q.reshape(B, nb, Q_BLOCK, H, QK_NOPE + QK_ROPE).transpose(1, 0, 2, 3, 4)
    kpos = jnp.arange(T)

    def block(args):
        qblk, start = args
        s = jnp.einsum('bqhd,bkhd->bhqk', qblk, k).astype(jnp.float32) * ATTN_SCALE
        qpos = start + jnp.arange(Q_BLOCK)
        s = jnp.where(kpos[None, :] <= qpos[:, None], s, -jnp.inf)
        p = jax.nn.softmax(s, axis=-1).astype(v.dtype)
        return jnp.einsum('bhqk,bkhd->bqhd', p, v)

    o = lax.map(block, (qb, jnp.arange(nb) * Q_BLOCK))
    return o.transpose(1, 0, 2, 3, 4).reshape(B, T, H, V_HEAD)


def mla_sample(q_nope, q_rope, ckv, kr, w_ukv, cache_ckv, cache_krope, page_table):
    B, T, H, _ = q_nope.shape
    w = w_ukv.reshape(KV_LORA, H, QK_NOPE + V_HEAD)
    w_uk, w_uv = w[..., :QK_NOPE], w[..., QK_NOPE:]
    q_lat = jnp.einsum('bthn,chn->bthc', q_nope, w_uk)
    causal = jnp.arange(T)[:, None, None] >= jnp.arange(T)[None, None, :]

    def one_seq(args):
        ql, qr, cn, kn, pt = args
        cp = cache_ckv[pt].reshape(-1, KV_LORA)
        kp = cache_krope[pt].reshape(-1, QK_ROPE)
        n_past = cp.shape[0]
        s_past = jnp.einsum('thc,pc->thp', ql, cp) + jnp.einsum('thr,pr->thp', qr, kp)
        s_new = jnp.einsum('thc,sc->ths', ql, cn) + jnp.einsum('thr,sr->ths', qr, kn)
        s = jnp.concatenate([s_past.astype(jnp.float32) * ATTN_SCALE,
                             jnp.where(causal, s_new.astype(jnp.float32) * ATTN_SCALE, -jnp.inf)], axis=-1)
        p = jax.nn.softmax(s, axis=-1).astype(cp.dtype)
        return (jnp.einsum('thp,pc->thc', p[..., :n_past], cp)
                + jnp.einsum('ths,sc->thc', p[..., n_past:], cn))

    o_lat = lax.map(one_seq, (q_lat, q_rope, ckv, kr, page_table))
    return jnp.einsum('bthc,chv->bthv', o_lat, w_uv)


def moe_dispatch(xt, expert_id, weight, w_gate, w_up, w_down):
    N, D = xt.shape
    A = N * TOP_K
    e_flat = expert_id.reshape(-1)
    w_flat = weight.reshape(-1)
    tok = jnp.repeat(jnp.arange(N, dtype=jnp.int32), TOP_K)
    order = jnp.argsort(e_flat)
    e_s, w_s, tok_s = e_flat[order], w_flat[order], tok[order]
    counts = jnp.zeros((N_EXPERTS,), jnp.int32).at[e_flat].add(1)
    start = jnp.cumsum(counts) - counts
    padded = (counts + ROUTE_BLOCK - 1) // ROUTE_BLOCK * ROUTE_BLOCK
    pend = jnp.cumsum(padded)
    pstart = pend - padded
    dest = pstart[e_s] + (jnp.arange(A, dtype=jnp.int32) - start[e_s])
    n_blocks = -(-A // ROUTE_BLOCK) + N_EXPERTS
    P = n_blocks * ROUTE_BLOCK
    buf_tok = jnp.full((P,), N, jnp.int32).at[dest].set(tok_s)
    buf_w = jnp.zeros((P,), xt.dtype).at[dest].set(w_s.astype(xt.dtype))
    blk_e = jnp.minimum(jnp.searchsorted(pend, jnp.arange(n_blocks, dtype=jnp.int32) * ROUTE_BLOCK, side='right'),
                        N_EXPERTS - 1)
    x_pad = jnp.concatenate([xt, jnp.zeros((1, D), xt.dtype)], axis=0)
    xb = x_pad[buf_tok].reshape(n_blocks, ROUTE_BLOCK, D)

    def run_block(args):
        xblk, e = args
        hdn = jax.nn.silu(xblk @ w_gate[e]) * (xblk @ w_up[e])
        return hdn @ w_down[e]

    yb = lax.map(run_block, (xb, blk_e)).reshape(P, D)
    y = jax.ops.segment_sum(yb * buf_w[:, None], buf_tok, num_segments=N + 1)
    return y[:N]


def hier_moe(h, router_group, router_expert, w_gate, w_up, w_down):
    B, T, D = h.shape
    xt = h.reshape(-1, D)
    N = xt.shape[0]
    g_logits = (xt @ router_group).astype(jnp.float32)
    g_prob = jax.nn.softmax(g_logits, axis=-1)
    g_sel = jnp.argmax(g_logits, axis=-1).astype(jnp.int32)
    g_w = jnp.take_along_axis(g_prob, g_sel[:, None], axis=-1)
    e_logits = (xt @ router_expert).astype(jnp.float32).reshape(N, N_GROUPS, EXPERTS_PER_GROUP)
    e_in = jnp.take_along_axis(e_logits, g_sel[:, None, None], axis=1)[:, 0]
    top_v, top_i = lax.top_k(e_in, TOP_K)
    top_w = jax.nn.softmax(top_v, axis=-1) * g_w
    expert_id = g_sel[:, None] * EXPERTS_PER_GROUP + top_i.astype(jnp.int32)
    y = moe_dispatch(xt, expert_id, top_w.astype(xt.dtype), w_gate, w_up, w_down)
    return y.reshape(B, T, D)


def hybrid_layer(x, c, pos, conv_past, attend, w_ada, b_ada, norm_mix_g, norm_ffn_g, w_in, conv_w,
                 q_norm_g, w_uq, kv_norm_g, w_ukv, w_o, router_group, router_expert, w_gate, w_up, w_down):
    B, T, _ = x.shape
    sh1, sc1, g1, sh2, sc2, g2 = ada_modulation(c, w_ada, b_ada)
    h = rmsnorm(x, norm_mix_g) * (1 + sc1) + sh1
    proj = h @ w_in
    v, gb, gc, cq, ckv, kr = jnp.split(proj, IN_SPLITS, axis=-1)
    u = gc * v
    u_pad = jnp.concatenate([conv_past, u], axis=1)
    conv = conv_w[0] * u_pad[:, 0:T]
    for k in range(1, CONV_W):
        conv = conv + conv_w[k] * u_pad[:, k:k + T]
    y_conv = gb * conv
    new_conv = u_pad[:, T:]
    q = (rmsnorm(cq, q_norm_g) @ w_uq).reshape(B, T, N_HEADS, QK_NOPE + QK_ROPE)
    q_nope = q[..., :QK_NOPE]
    q_rope = rope(q[..., QK_NOPE:], pos)
    ckv = rmsnorm(ckv, kv_norm_g)
    kr = rope(kr[:, :, None, :], pos)[:, :, 0]
    y_attn = attend(q_nope, q_rope, ckv, kr, w_ukv).reshape(B, T, N_HEADS * V_HEAD)
    x = x + g1 * (jnp.concatenate([y_conv, y_attn], axis=-1) @ w_o)
    h2 = rmsnorm(x, norm_ffn_g) * (1 + sc2) + sh2
    x = x + g2 * hier_moe(h2, router_group, router_expert, w_gate, w_up, w_down)
    return x, ckv, kr, new_conv


def setup_inputs(seed: int = 0) -> dict:
    key = jax.random.key(seed)
    ks = iter(jax.random.split(key, 32))
    n_pages = PAST_LEN // PAGE_SIZE
    n_pool = (DEC_BATCH * n_pages * 5) // 4
    L = DEPTH

    def nrm(shape, scale):
        return jax.random.normal(next(ks), shape, jnp.float32) * scale

    def gain(shape):
        return 1.0 + nrm(shape, 0.05)

    x_prompt = nrm((BATCH, SEQ, D_MODEL), 1.0)
    x_sample = nrm((DEC_BATCH, DEC_SEQ, D_MODEL), 1.0)
    cache_ckv = nrm((L, n_pool, PAGE_SIZE, KV_LORA), 1.0)
    cache_krope = nrm((L, n_pool, PAGE_SIZE, QK_ROPE), 1.0)
    state_conv = nrm((L, DEC_BATCH, CONV_W - 1, D_CONV), 1.0)
    page_table = jax.random.permutation(next(ks), n_pool)[:DEC_BATCH * n_pages].reshape(DEC_BATCH, n_pages).astype(jnp.int32)
    c_prompt = nrm((BATCH, D_MODEL), 1.0)
    c_sample = nrm((DEC_BATCH, D_MODEL), 1.0)
    return {
        'x_prompt': x_prompt,
        'x_sample': x_sample,
        'cache_ckv': cache_ckv,
        'cache_krope': cache_krope,
        'state_conv': state_conv,
        'page_table': page_table,
        'c_prompt': c_prompt,
        'c_sample': c_sample,
        'w_ada': nrm((L, D_MODEL, 6 * D_MODEL), 0.5 * D_MODEL ** -0.5),
        'b_ada': nrm((L, 6 * D_MODEL), 0.01),
        'norm_mix_g': gain((L, D_MODEL)),
        'norm_ffn_g': gain((L, D_MODEL)),
        'w_in': nrm((L, D_MODEL, D_IN), D_MODEL ** -0.5),
        'conv_w': nrm((L, CONV_W, D_CONV), CONV_W ** -0.5),
        'q_norm_g': gain((L, Q_LORA)),
        'w_uq': nrm((L, Q_LORA, N_HEADS * (QK_NOPE + QK_ROPE)), Q_LORA ** -0.5),
        'kv_norm_g': gain((L, KV_LORA)),
        'w_ukv': nrm((L, KV_LORA, N_HEADS * (QK_NOPE + V_HEAD)), KV_LORA ** -0.5),
        'w_o': nrm((L, D_MIX, D_MODEL), D_MIX ** -0.5),
        'router_group': nrm((L, D_MODEL, N_GROUPS), D_MODEL ** -0.5),
        'router_expert': nrm((L, D_MODEL, N_EXPERTS), D_MODEL ** -0.5),
        'w_gate': nrm((L, N_EXPERTS, D_MODEL, D_EXPERT), D_MODEL ** -0.5),
        'w_up': nrm((L, N_EXPERTS, D_MODEL, D_EXPERT), D_MODEL ** -0.5),
        'w_down': nrm((L, N_EXPERTS, D_EXPERT, D_MODEL), D_EXPERT ** -0.5),
        'final_g': gain((D_MODEL,)),
    }


def reference(x_prompt, x_sample, cache_ckv, cache_krope, state_conv, page_table, c_prompt, c_sample,
              w_ada, b_ada, norm_mix_g, norm_ffn_g, w_in, conv_w, q_norm_g, w_uq, kv_norm_g, w_ukv, w_o,
              router_group, router_expert, w_gate, w_up, w_down, final_g):
    T_p = x_prompt.shape[1]
    T_s = x_sample.shape[1]
    past_len = page_table.shape[1] * PAGE_SIZE
    pos_p = jnp.arange(T_p, dtype=jnp.int32)
    pos_s = past_len + jnp.arange(T_s, dtype=jnp.int32)
    xp, xs = x_prompt, x_sample
    ckv_p, kr_p, conv_p, ckv_s, kr_s, conv_s = [], [], [], [], [], []
    for l in range(DEPTH):
        weights = (w_ada[l], b_ada[l], norm_mix_g[l], norm_ffn_g[l], w_in[l], conv_w[l], q_norm_g[l], w_uq[l],
                   kv_norm_g[l], w_ukv[l], w_o[l], router_group[l], router_expert[l], w_gate[l], w_up[l], w_down[l])
        zero_past = jnp.zeros((xp.shape[0], CONV_W - 1, D_CONV), xp.dtype)
        xp, a, b, cst = hybrid_layer(xp, c_prompt, pos_p, zero_past, mla_prompt, *weights)
        ckv_p.append(a)
        kr_p.append(b)
        conv_p.append(cst)
        attend_s = functools.partial(mla_sample, cache_ckv=cache_ckv[l], cache_krope=cache_krope[l],
                                     page_table=page_table)
        xs, a, b, cst = hybrid_layer(xs, c_sample, pos_s, state_conv[l], attend_s, *weights)
        ckv_s.append(a)
        kr_s.append(b)
        conv_s.append(cst)
    y_prompt = rmsnorm(xp, final_g)
    y_sample = rmsnorm(xs, final_g)
    return (y_prompt, y_sample, jnp.stack(ckv_p), jnp.stack(kr_p), jnp.stack(conv_p),
            jnp.stack(ckv_s), jnp.stack(kr_s), jnp.stack(conv_s))
```

```python
import functools

import jax
import jax.numpy as jnp
from jax import lax
from jax.experimental import pallas as pl
from jax.experimental.pallas import tpu as pltpu

F32, BF16, I32 = jnp.float32, jnp.bfloat16, jnp.int32

N_HEADS = 8
QK_NOPE = 128
QK_ROPE = 64
V_HEAD = 128
ROPE_THETA = 10000.0
EPS = 1e-6
ATTN_SCALE = (QK_NOPE + QK_ROPE) ** -0.5
ROUTE_BLOCK = 128
N_MOD = 6

LANE = 128
SUBLANE = 8
VMEM_LIMIT = 56 * 1024 * 1024

ROPE_PAD = LANE
DQK = QK_NOPE + ROPE_PAD


def _params(*sem):
    return pltpu.CompilerParams(dimension_semantics=sem, vmem_limit_bytes=VMEM_LIMIT)


def _resident(shape):
    nd = len(shape)
    return pl.BlockSpec(shape, lambda *_: (0,) * nd, pipeline_mode=pl.Buffered(1))


def _dot(a, b):
    return jnp.dot(a, b, preferred_element_type=F32)


def _dot_t(a, b):
    return lax.dot_general(a, b, (((1,), (1,)), ((), ())), preferred_element_type=F32)


def _mm(a, w):
    return _dot(a.astype(BF16), w.astype(BF16))


def _rms(x, g):
    return x * lax.rsqrt(jnp.mean(x * x, axis=-1, keepdims=True) + EPS) * g


def _rope(x, cos, sin):
    n = x.shape[1] // LANE
    if n > 1:
        cos = jnp.concatenate([cos] * n, axis=1)
        sin = jnp.concatenate([sin] * n, axis=1)
    lane = lax.broadcasted_iota(I32, x.shape, 1)
    first_half = (lane % LANE) < (QK_ROPE // 2)
    partner = jnp.where(first_half,
                        pltpu.roll(x, x.shape[1] - QK_ROPE // 2, 1),
                        pltpu.roll(x, QK_ROPE // 2, 1))
    return x * cos + partner * sin


def _mod_row(ref, per_row):
    return ref[...] if per_row else ref[0:1, :]


def _ada_kernel(c_ref, w_ref, b_ref, o_ref):
    o_ref[...] = _mm(jax.nn.silu(c_ref[...]), w_ref[...]) + b_ref[...]


def _ada(c_all, w_ada, b_ada):
    m, d = c_all.shape
    n = w_ada.shape[1]
    tn = 512
    return pl.pallas_call(
        _ada_kernel,
        grid=(n // tn,),
        in_specs=[pl.BlockSpec((m, d), lambda j: (0, 0)),
                  pl.BlockSpec((d, tn), lambda j: (0, j)),
                  pl.BlockSpec((1, tn), lambda j: (0, j))],
        out_specs=pl.BlockSpec((m, tn), lambda j: (0, j)),
        out_shape=jax.ShapeDtypeStruct((m, n), F32),
        compiler_params=_params("arbitrary"),
        name="ada_mod",
    )(c_all, w_ada, b_ada)


def _inproj_kernel(*refs, sample, tm, d_conv, q_lora, kv_lora):
    it = iter(refs)
    x_ref, sh_ref, sc_ref, g_ref = next(it), next(it), next(it), next(it)
    win = next(it)
    cw_ref = next(it)
    if sample:
        st0_ref, st1_ref = next(it), next(it)
    qg_ref, kvg_ref = next(it), next(it)
    wuq = next(it)
    wukv = None if sample else next(it)
    cos_ref, sin_ref = next(it), next(it)
    if sample:
        yconv_ref, qn_ref, qr_ref, ckv_ref, kr_ref, cn0_ref, cn1_ref = it
    else:
        yconv_ref, q_ref, k_ref, v_ref, ckv_ref, kr_ref, cst_ref, carry_ref = it

    def proj(a, lo, hi_col):
        return _mm(a, win[:, lo:hi_col])

    x = x_ref[...]
    h = _rms(x, g_ref[...]) * (1.0 + _mod_row(sc_ref, sample)) + _mod_row(sh_ref, sample)

    c1, c2, c3 = d_conv, 2 * d_conv, 3 * d_conv
    u = proj(h, c2, c3) * proj(h, 0, c1)
    if sample:
        u1, u2 = st1_ref[...], st0_ref[...]
        cn0_ref[...] = u1
        cn1_ref[...] = u
    else:
        @pl.when(pl.program_id(0) == 0)
        def _():
            carry_ref[...] = jnp.zeros_like(carry_ref)
        prev = carry_ref[...]
        p1, p2 = prev[SUBLANE - 1:SUBLANE, :], prev[SUBLANE - 2:SUBLANE - 1, :]
        rows = lax.broadcasted_iota(I32, u.shape, 0)
        u1 = jnp.where(rows == 0, p1, pltpu.roll(u, 1, 0))
        u2 = jnp.where(rows == 0, p2, jnp.where(rows == 1, p1, pltpu.roll(u, 2, 0)))
        tail = u[tm - SUBLANE:tm, :]
        carry_ref[...] = tail
        cst_ref[...] = tail
    conv = cw_ref[0:1, :] * u2 + cw_ref[1:2, :] * u1 + cw_ref[2:3, :] * u
    yconv_ref[...] = (proj(h, c1, c2) * conv).astype(yconv_ref.dtype)

    cos, sin = cos_ref[...], sin_ref[...]
    c4, c5 = c3 + q_lora, c3 + q_lora + kv_lora
    cqn = _rms(proj(h, c3, c4), qg_ref[...])
    ckv = _rms(proj(h, c4, c5), kvg_ref[...])
    kr = _rope(proj(h, c5, c5 + LANE), cos, sin)
    ckv_ref[...] = ckv
    kr_ref[...] = kr[:, :QK_ROPE]
    hq = N_HEADS * QK_NOPE
    q = _mm(cqn, wuq[...])
    if sample:
        qn_ref[...] = q[:, :hq]
        qr_ref[...] = _rope(q[:, hq:], cos, sin)
    else:
        q = q * ATTN_SCALE
        qn = q[:, :hq]
        qr = _rope(q[:, hq:], cos, sin)
        kv = _mm(ckv, wukv[...])
        krb = kr.astype(BF16)
        for hd in range(N_HEADS):
            lo, hi = hd * LANE, (hd + 1) * LANE
            q_ref[hd, :, 0:QK_NOPE] = qn[:, lo:hi].astype(BF16)
            q_ref[hd, :, QK_NOPE:DQK] = qr[:, lo:hi].astype(BF16)
            k_ref[hd, :, 0:QK_NOPE] = kv[:, lo:hi].astype(BF16)
            k_ref[hd, :, QK_NOPE:DQK] = krb
            v_ref[hd] = kv[:, hq + lo:hq + hi].astype(BF16)


def _inproj(x, mod, mod_row_block, norm_g, win, conv_w, state, q_g, kv_g, wuq, wukv, cos, sin, *,
            sample, tm):
    n, d = x.shape
    d_conv = conv_w.shape[1]
    q_lora, kv_lora = q_g.shape[1], kv_g.shape[1]
    rows = tm if sample else SUBLANE
    hq = N_HEADS * QK_NOPE

    def tile(width):
        return pl.BlockSpec((tm, width), lambda i: (i, 0))

    def modspec(col):
        return pl.BlockSpec((rows, d), lambda i: (mod_row_block, col))

    args = [x, mod, mod, norm_g, win, conv_w]
    specs = [tile(d), modspec(0), modspec(1), _resident(norm_g.shape), _resident(win.shape),
             _resident(conv_w.shape)]
    if sample:
        args += list(state)
        specs += [tile(d_conv), tile(d_conv)]
    args += [q_g, kv_g, wuq]
    specs += [_resident(q_g.shape), _resident(kv_g.shape), _resident(wuq.shape)]
    if not sample:
        args.append(wukv)
        specs.append(_resident(wukv.shape))
    args += [cos, sin]
    specs += [tile(LANE), tile(LANE)]

    if sample:
        out_shape = [jax.ShapeDtypeStruct((n, d_conv), F32), jax.ShapeDtypeStruct((n, hq), F32),
                     jax.ShapeDtypeStruct((n, N_HEADS * ROPE_PAD), F32),
                     jax.ShapeDtypeStruct((n, kv_lora), F32), jax.ShapeDtypeStruct((n, QK_ROPE), F32),
                     jax.ShapeDtypeStruct((n, d_conv), F32), jax.ShapeDtypeStruct((n, d_conv), F32)]
        out_specs = [tile(d_conv), tile(hq), tile(N_HEADS * ROPE_PAD), tile(kv_lora), tile(QK_ROPE),
                     tile(d_conv), tile(d_conv)]
        scratch = []
    else:
        def heads(width):
            return pl.BlockSpec((N_HEADS, tm, width), lambda i: (0, i, 0))
        out_shape = [jax.ShapeDtypeStruct((n, d_conv), BF16),
                     jax.ShapeDtypeStruct((N_HEADS, n, DQK), BF16),
                     jax.ShapeDtypeStruct((N_HEADS, n, DQK), BF16),
                     jax.ShapeDtypeStruct((N_HEADS, n, V_HEAD), BF16),
                     jax.ShapeDtypeStruct((n, kv_lora), F32), jax.ShapeDtypeStruct((n, QK_ROPE), F32),
                     jax.ShapeDtypeStruct((SUBLANE, d_conv), F32)]
        out_specs = [tile(d_conv), heads(DQK), heads(DQK), heads(V_HEAD), tile(kv_lora), tile(QK_ROPE),
                     pl.BlockSpec((SUBLANE, d_conv), lambda i: (0, 0))]
        scratch = [pltpu.VMEM((SUBLANE, d_conv), F32)]

    return pl.pallas_call(
        functools.partial(_inproj_kernel, sample=sample, tm=tm, d_conv=d_conv, q_lora=q_lora,
                          kv_lora=kv_lora),
        grid=(n // tm,),
        in_specs=specs, out_specs=out_specs, out_shape=out_shape, scratch_shapes=scratch,
        compiler_params=_params("arbitrary"),
        name="inproj_sample" if sample else "inproj_prompt",
    )(*args)


def _flash_kernel(q_ref, k_ref, v_ref, o_ref, m_sc, l_sc, acc_sc, *, tq):
    qi = pl.program_id(1)
    q = q_ref[0]
    m_sc[...] = jnp.full_like(m_sc, -jnp.inf)
    l_sc[...] = jnp.zeros_like(l_sc)
    acc_sc[...] = jnp.zeros_like(acc_sc)

    def step(ki, masked):
        off = pl.multiple_of(ki * tq, tq)
        k = k_ref[0, pl.ds(off, tq), :]
        v = v_ref[0, pl.ds(off, tq), :]
        s = _dot_t(q, k)
        if masked:
            row = lax.broadcasted_iota(I32, s.shape, 0)
            col = lax.broadcasted_iota(I32, s.shape, 1)
            s = jnp.where(col <= row, s, -jnp.inf)
        m_prev = m_sc[...]
        m_new = jnp.maximum(m_prev, jnp.max(s, axis=1, keepdims=True))
        alpha = jnp.exp(m_prev - m_new)
        p = jnp.exp(s - m_new)
        l_sc[...] = alpha * l_sc[...] + jnp.sum(p, axis=1, keepdims=True)
        acc_sc[...] = alpha * acc_sc[...] + _dot(p.astype(BF16), v)
        m_sc[...] = m_new

    def body(ki, carry):
        step(ki, False)
        return carry

    lax.fori_loop(0, qi, body, 0)
    step(qi, True)
    o_ref[...] = (acc_sc[...] / l_sc[...]).astype(o_ref.dtype)


def _flash(q, k, v, *, tq):
    h, t, _ = q.shape
    return pl.pallas_call(
        functools.partial(_flash_kernel, tq=tq),
        grid=(h, t // tq),
        in_specs=[pl.BlockSpec((1, tq, DQK), lambda hh, i: (hh, i, 0)),
                  pl.BlockSpec((1, t, DQK), lambda hh, i: (hh, 0, 0)),
                  pl.BlockSpec((1, t, V_HEAD), lambda hh, i: (hh, 0, 0))],
        out_specs=pl.BlockSpec((tq, V_HEAD), lambda hh, i: (i, hh)),
        out_shape=jax.ShapeDtypeStruct((t, h * V_HEAD), BF16),
        scratch_shapes=[pltpu.VMEM((tq, 1), F32), pltpu.VMEM((tq, 1), F32),
                        pltpu.VMEM((tq, V_HEAD), F32)],
        compiler_params=_params("arbitrary", "arbitrary"),
        name="flash_prompt",
    )(q, k, v)


def _qlat_kernel(qn_ref, w_ref, o_ref):
    o_ref[...] = _dot_t(qn_ref[...].astype(BF16), w_ref[:, :QK_NOPE].astype(BF16))


def _qlat(qn, w_ukv):
    n = qn.shape[0]
    kv_lora = w_ukv.shape[0]
    per_head = QK_NOPE + V_HEAD
    return pl.pallas_call(
        _qlat_kernel,
        grid=(N_HEADS,),
        in_specs=[pl.BlockSpec((n, QK_NOPE), lambda h: (0, h)),
                  pl.BlockSpec((kv_lora, per_head), lambda h: (0, h))],
        out_specs=pl.BlockSpec((n, kv_lora), lambda h: (0, h)),
        out_shape=jax.ShapeDtypeStruct((n, N_HEADS * kv_lora), F32),
        compiler_params=_params("arbitrary"),
        name="sample_qlat",
    )(qn, w_ukv)


def _uv_kernel(o_ref_in, w_ref, y_ref):
    y_ref[...] = _mm(o_ref_in[...], w_ref[:, QK_NOPE:])


def _uv(o_lat, w_ukv):
    n = o_lat.shape[0]
    kv_lora = w_ukv.shape[0]
    per_head = QK_NOPE + V_HEAD
    return pl.pallas_call(
        _uv_kernel,
        grid=(N_HEADS,),
        in_specs=[pl.BlockSpec((n, kv_lora), lambda h: (0, h)),
                  pl.BlockSpec((kv_lora, per_head), lambda h: (0, h))],
        out_specs=pl.BlockSpec((n, V_HEAD), lambda h: (0, h)),
        out_shape=jax.ShapeDtypeStruct((n, N_HEADS * V_HEAD), F32),
        compiler_params=_params("arbitrary"),
        name="sample_uv",
    )(o_lat, w_ukv)


def _decode_kernel(pt_ref, ql_ref, qr_ref, cn_ref, kn_ref, ckv_hbm, kr_hbm, o_ref,
                   ckv_buf, kr_buf, sem, m_sc, l_sc, acc_sc, *, pages, page):
    s_idx, j = pl.program_id(0), pl.program_id(1)
    n_seq, nj = pl.num_programs(0), pl.num_programs(1)
    step = s_idx * nj + j

    def chunk_copies(seq, chunk, slot):
        out = []
        for p in range(pages):
            pid = pt_ref[seq, chunk * pages + p]
            rows = pl.ds(p * page, page)
            out.append(pltpu.make_async_copy(ckv_hbm.at[pid], ckv_buf.at[slot, rows], sem.at[slot, 0]))
            out.append(pltpu.make_async_copy(kr_hbm.at[pid], kr_buf.at[slot, rows], sem.at[slot, 1]))
        return out

    @pl.when(step == 0)
    def _():
        for c in chunk_copies(0, 0, 0):
            c.start()

    @pl.when(step + 1 < n_seq * nj)
    def _():
        wrap = j + 1 == nj
        for c in chunk_copies(jnp.where(wrap, s_idx + 1, s_idx), jnp.where(wrap, 0, j + 1),
                              (step + 1) % 2):
            c.start()

    slot = step % 2
    for c in chunk_copies(s_idx, j, slot):
        c.wait()

    ql = ql_ref[0].astype(BF16).astype(F32)
    qr = qr_ref[0].astype(BF16).astype(F32)

    @pl.when(j == 0)
    def _():
        m_sc[...] = jnp.full_like(m_sc, -jnp.inf)
        l_sc[...] = jnp.zeros_like(l_sc)
        acc_sc[...] = jnp.zeros_like(acc_sc)

    ckv = ckv_buf[slot]
    s = (_dot_t(ql, ckv) + _dot_t(qr, kr_buf[slot])) * ATTN_SCALE
    m_prev = m_sc[...]
    m_new = jnp.maximum(m_prev, jnp.max(s, axis=1, keepdims=True))
    alpha = jnp.exp(m_prev - m_new)
    pr = jnp.exp(s - m_new)
    l_new = alpha * l_sc[...] + jnp.sum(pr, axis=1, keepdims=True)
    acc = alpha * acc_sc[...] + _dot(pr, ckv)
    m_sc[...] = m_new
    l_sc[...] = l_new
    acc_sc[...] = acc

    @pl.when(j == pl.num_programs(1) - 1)
    def _():
        cn = cn_ref[0].astype(BF16).astype(F32)
        kn = kn_ref[0].astype(BF16).astype(F32)
        s_new = (jnp.sum(ql * cn, axis=1, keepdims=True)
                 + jnp.sum(qr * kn, axis=1, keepdims=True)) * ATTN_SCALE
        m_fin = jnp.maximum(m_new, s_new)
        a_fin = jnp.exp(m_new - m_fin)
        p_new = jnp.exp(s_new - m_fin)
        l_fin = a_fin * l_new + p_new
        o_ref[0] = (a_fin * acc + p_new * cn) / l_fin


def _decode(page_table, q_lat, q_rope, ckv_new, kr_new, cache_ckv, cache_kr, *, pages):
    b, n_pages = page_table.shape
    h, kv_lora = q_lat.shape[1:]
    rope = q_rope.shape[2]
    page = cache_ckv.shape[1]

    assert n_pages % pages == 0
    in_specs = [pl.BlockSpec((1, h, kv_lora), lambda s, j, pt: (s, 0, 0)),
                pl.BlockSpec((1, h, rope), lambda s, j, pt: (s, 0, 0)),
                pl.BlockSpec((1, 1, kv_lora), lambda s, j, pt: (s, 0, 0)),
                pl.BlockSpec((1, 1, rope), lambda s, j, pt: (s, 0, 0)),
                pl.BlockSpec(memory_space=pl.ANY),
                pl.BlockSpec(memory_space=pl.ANY)]
    grid_spec = pltpu.PrefetchScalarGridSpec(
        num_scalar_prefetch=1,
        grid=(b, n_pages // pages),
        in_specs=in_specs,
        out_specs=pl.BlockSpec((1, h, kv_lora), lambda s, j, pt: (s, 0, 0)),
        scratch_shapes=[pltpu.VMEM((2, pages * page, kv_lora), F32),
                        pltpu.VMEM((2, pages * page, rope), F32),
                        pltpu.SemaphoreType.DMA((2, 2)),
                        pltpu.VMEM((h, 1), F32), pltpu.VMEM((h, 1), F32),
                        pltpu.VMEM((h, kv_lora), F32)])
    return pl.pallas_call(
        functools.partial(_decode_kernel, pages=pages, page=page),
        grid_spec=grid_spec,
        out_shape=jax.ShapeDtypeStruct((b, h, kv_lora), F32),
        compiler_params=_params("arbitrary", "arbitrary"),
        name="decode_sample",
    )(page_table, q_lat, q_rope, ckv_new, kr_new, cache_ckv, cache_kr)


def _oproj_kernel(*refs, sample, tm, d_conv, n_experts, n_groups):
    it = iter(refs)
    x_ref, yc_ref, ya_ref, g1_ref, sh_ref, sc_ref, ng_ref = (next(it) for _ in range(7))
    wo, router, cnt_in_ref = next(it), next(it), next(it)
    x1_ref, h2_ref, info_ref, cnt_ref, carry_ref = it

    d_mix = wo.shape[0]
    o = _mm(yc_ref[...], wo[0:d_conv, :]) + _mm(ya_ref[...], wo[d_conv:d_mix, :])
    x1 = x_ref[...] + _mod_row(g1_ref, sample) * o
    x1_ref[...] = x1
    h2 = _rms(x1, ng_ref[...]) * (1.0 + _mod_row(sc_ref, sample)) + _mod_row(sh_ref, sample)
    h2_ref[...] = h2

    lg = _mm(h2, router[...])
    lane_i = lax.broadcasted_iota(I32, lg.shape, 1)
    lane = lane_i.astype(F32)
    big = float(LANE)
    neg = -jnp.inf
    gmask = (lane_i >= n_experts) & (lane_i < n_experts + n_groups)
    gl = jnp.where(gmask, lg, neg)
    gmax = jnp.max(gl, axis=1, keepdims=True)
    g_sel = jnp.min(jnp.where(gl == gmax, lane, big), axis=1, keepdims=True) - float(n_experts)
    g_w = 1.0 / jnp.sum(jnp.where(gmask, jnp.exp(lg - gmax), 0.0), axis=1, keepdims=True)
    per_group = n_experts // n_groups
    emask = (lane_i < n_experts) & ((lane_i // per_group).astype(F32) == g_sel)
    el = jnp.where(emask, lg, neg)
    v1 = jnp.max(el, axis=1, keepdims=True)
    i1 = jnp.min(jnp.where(el == v1, lane, big), axis=1, keepdims=True)
    el2 = jnp.where(lane == i1, neg, el)
    v2 = jnp.max(el2, axis=1, keepdims=True)
    i2 = jnp.min(jnp.where(el2 == v2, lane, big), axis=1, keepdims=True)
    e2 = jnp.exp(v2 - v1)
    w1 = (1.0 / (1.0 + e2)) * g_w
    w2 = (e2 / (1.0 + e2)) * g_w

    @pl.when(pl.program_id(0) == 0)
    def _():
        carry_ref[...] = cnt_in_ref[...]
    hit1, hit2 = lane == i1, lane == i2
    chosen = jnp.where(hit1 | hit2, 1.0, 0.0)
    row = lax.broadcasted_iota(I32, (tm, tm), 0)
    col = lax.broadcasted_iota(I32, (tm, tm), 1)
    before = jnp.where(col < row, 1.0, 0.0).astype(BF16)
    running = _dot(before, chosen.astype(BF16)) + carry_ref[0:1, :]
    rank1 = jnp.sum(jnp.where(hit1, running, 0.0), axis=1, keepdims=True)
    rank2 = jnp.sum(jnp.where(hit2, running, 0.0), axis=1, keepdims=True)
    total = carry_ref[...] + jnp.sum(chosen, axis=0, keepdims=True)
    carry_ref[...] = total
    cnt_ref[...] = total

    info = jnp.where(lane_i == 0, i1, 0.0)
    info = jnp.where(lane_i == 1, i2, info)
    info = jnp.where(lane_i == 2, rank1, info)
    info = jnp.where(lane_i == 3, rank2, info)
    info = jnp.where(lane_i == 4, w1, info)
    info = jnp.where(lane_i == 5, w2, info)
    info_ref[...] = info


def _oproj(x, yconv, yattn, mod, mod_row_block, norm_g, wo, router, cnt_in, *, sample, tm,
           n_experts, n_groups):
    n, d = x.shape
    d_conv = yconv.shape[1]
    rows = tm if sample else SUBLANE

    def tile(width):
        return pl.BlockSpec((tm, width), lambda i: (i, 0))

    def modspec(col):
        return pl.BlockSpec((rows, d), lambda i: (mod_row_block, col))

    args = [x, yconv, yattn, mod, mod, mod, norm_g]
    specs = [tile(d), tile(d_conv), tile(yattn.shape[1]), modspec(2), modspec(3), modspec(4),
             _resident(norm_g.shape)]
    for w in (wo, router, cnt_in):
        args.append(w)
        specs.append(_resident(w.shape))
    return pl.pallas_call(
        functools.partial(_oproj_kernel, sample=sample, tm=tm, d_conv=d_conv, n_experts=n_experts,
                          n_groups=n_groups),
        grid=(n // tm,),
        in_specs=specs,
        out_specs=[tile(d), tile(d), tile(LANE), pl.BlockSpec((SUBLANE, LANE), lambda i: (0, 0))],
        out_shape=[jax.ShapeDtypeStruct((n, d), F32), jax.ShapeDtypeStruct((n, d), F32),
                   jax.ShapeDtypeStruct((n, LANE), F32), jax.ShapeDtypeStruct((SUBLANE, LANE), F32)],
        scratch_shapes=[pltpu.VMEM((SUBLANE, LANE), F32)],
        compiler_params=_params("arbitrary"),
        name="oproj_sample" if sample else "oproj_prompt",
    )(*args)


def _row_copy(src, src_row, dst, dst_row, sem):
    return pltpu.make_async_copy(src.at[pl.ds(src_row, 1)], dst.at[pl.ds(dst_row, 1)], sem)


def _scatter_kernel(dest_ref, fill_ref, hp_ref, hs_ref, xs_ref, zero_ref, sem, *, tm, n_prompt_tiles):
    i = pl.program_id(0)

    def scatter_rows(h_ref, n_rows, base):
        def start(r, carry):
            t = base + r
            _row_copy(h_ref, r, xs_ref, dest_ref[2 * t], sem).start()
            _row_copy(h_ref, r, xs_ref, dest_ref[2 * t + 1], sem).start()
            return carry

        def wait(r, carry):
            _row_copy(h_ref, 0, xs_ref, 0, sem).wait()
            _row_copy(h_ref, 0, xs_ref, 0, sem).wait()
            return carry

        lax.fori_loop(0, n_rows, start, 0)
        lax.fori_loop(0, n_rows, wait, 0)

    @pl.when(i < n_prompt_tiles)
    def _():
        scatter_rows(hp_ref, tm, i * tm)

    @pl.when(i == n_prompt_tiles)
    def _():
        scatter_rows(hs_ref, hs_ref.shape[0], n_prompt_tiles * tm)

    @pl.when(i == 0)
    def _():
        zero_ref[...] = jnp.zeros_like(zero_ref)

        def each_range(act):
            def per_range(e, carry):
                first, count = fill_ref[2 * e], fill_ref[2 * e + 1]
                lax.fori_loop(0, count, lambda r, c: act(first + r, c), 0)
                return carry
            lax.fori_loop(0, fill_ref.shape[0] // 2, per_range, 0)

        def zstart(slot, c):
            _row_copy(zero_ref, 0, xs_ref, slot, sem).start()
            return c

        def zwait(slot, c):
            _row_copy(zero_ref, 0, xs_ref, 0, sem).wait()
            return c

        each_range(zstart)
        each_range(zwait)


def _scatter(dest, fill, h2_p, h2_s, n_slots, *, tm):
    n_p, d = h2_p.shape
    n_prompt_tiles = n_p // tm
    grid_spec = pltpu.PrefetchScalarGridSpec(
        num_scalar_prefetch=2,
        grid=(n_prompt_tiles + 1,),
        in_specs=[pl.BlockSpec((tm, d), lambda i, *_: (jnp.minimum(i, n_prompt_tiles - 1), 0)),
                  pl.BlockSpec(h2_s.shape, lambda i, *_: (0, 0))],
        out_specs=pl.BlockSpec(memory_space=pl.ANY),
        scratch_shapes=[pltpu.VMEM((SUBLANE, d), F32), pltpu.SemaphoreType.DMA(())])
    return pl.pallas_call(
        functools.partial(_scatter_kernel, tm=tm, n_prompt_tiles=n_prompt_tiles),
        grid_spec=grid_spec,
        out_shape=jax.ShapeDtypeStruct((n_slots, d), F32),
        compiler_params=_params("arbitrary"),
        name="moe_scatter",
    )(dest, fill, h2_p, h2_s)


def _ffn_kernel(blk_e_ref, nused_ref, xs_ref, wg_ref, wu_ref, wd_ref, o_ref):
    del blk_e_ref

    used = pl.program_id(0) < nused_ref[0]

    @pl.when(used)
    def _():
        x = xs_ref[...]
        hdn = jax.nn.silu(_dot(x, wg_ref[0])) * _dot(x, wu_ref[0])
        o_ref[...] = _dot(hdn, wd_ref[0])

    @pl.when(jnp.logical_not(used))
    def _():
        o_ref[...] = jnp.zeros_like(o_ref)


def _ffn(blk_e, nused, xs, w_gate, w_up, w_down):
    n_slots, d = xs.shape
    n_blocks = n_slots // ROUTE_BLOCK
    d_e = w_gate.shape[2]

    def rows(b, be, nu):
        return (jnp.minimum(b, nu[0] - 1), 0)

    grid_spec = pltpu.PrefetchScalarGridSpec(
        num_scalar_prefetch=2,
        grid=(n_blocks,),
        in_specs=[pl.BlockSpec((ROUTE_BLOCK, d), rows),
                  pl.BlockSpec((1, d, d_e), lambda b, be, nu: (be[b], 0, 0)),
                  pl.BlockSpec((1, d, d_e), lambda b, be, nu: (be[b], 0, 0)),
                  pl.BlockSpec((1, d_e, d), lambda b, be, nu: (be[b], 0, 0))],
        out_specs=pl.BlockSpec((ROUTE_BLOCK, d), lambda b, be, nu: (b, 0)))
    return pl.pallas_call(
        _ffn_kernel,
        grid_spec=grid_spec,
        out_shape=jax.ShapeDtypeStruct((n_slots, d), F32),
        compiler_params=_params("arbitrary"),
        name="moe_experts",
    )(blk_e, nused, xs, w_gate, w_up, w_down)


def _final_kernel(dest_ref, x1_ref, info_ref, g2_ref, fg_ref, yb_ref, o_ref, rows_ref, sem, *,
                  tm, base, per_row):
    i = pl.program_id(0)
    n = pl.num_programs(0)

    def issue(tile_idx, slot):
        def start(r, carry):
            t = base + tile_idx * tm + r
            _row_copy(yb_ref, dest_ref[2 * t], rows_ref.at[slot, 0], r, sem.at[slot]).start()
            _row_copy(yb_ref, dest_ref[2 * t + 1], rows_ref.at[slot, 1], r, sem.at[slot]).start()
            return carry
        lax.fori_loop(0, tm, start, 0)

    @pl.when(i == 0)
    def _():
        issue(0, 0)

    @pl.when(i + 1 < n)
    def _():
        issue(i + 1, (i + 1) % 2)

    slot = i % 2

    def wait(r, carry):
        _row_copy(yb_ref, 0, rows_ref.at[slot, 0], 0, sem.at[slot]).wait()
        _row_copy(yb_ref, 0, rows_ref.at[slot, 1], 0, sem.at[slot]).wait()
        return carry

    lax.fori_loop(0, tm, wait, 0)
    info = info_ref[...]
    moe = info[:, 4:5] * rows_ref[slot, 0] + info[:, 5:6] * rows_ref[slot, 1]
    y = x1_ref[...] + _mod_row(g2_ref, per_row) * moe
    o_ref[...] = _rms(y, fg_ref[...])


def _final(dest, x1, info, mod, mod_row_block, final_g, yb, *, tm, base, per_row):
    n, d = x1.shape
    rows = tm if per_row else SUBLANE
    grid_spec = pltpu.PrefetchScalarGridSpec(
        num_scalar_prefetch=1,
        grid=(n // tm,),
        in_specs=[pl.BlockSpec((tm, d), lambda i, *_: (i, 0)),
                  pl.BlockSpec((tm, LANE), lambda i, *_: (i, 0)),
                  pl.BlockSpec((rows, d), lambda i, *_: (mod_row_block, N_MOD - 1)),
                  pl.BlockSpec((1, d), lambda i, *_: (0, 0)),
                  pl.BlockSpec(memory_space=pl.ANY)],
        out_specs=pl.BlockSpec((tm, d), lambda i, *_: (i, 0)),
        scratch_shapes=[pltpu.VMEM((2, 2, tm, d), F32), pltpu.SemaphoreType.DMA((2,))])
    return pl.pallas_call(
        functools.partial(_final_kernel, tm=tm, base=base, per_row=per_row),
        grid_spec=grid_spec,
        out_shape=jax.ShapeDtypeStruct((n, d), F32),
        compiler_params=_params("arbitrary"),
        name="moe_combine_final",
    )(dest, x1, info, mod, final_g, yb)


def _rope_tables(pos):
    half = QK_ROPE // 2
    inv_freq = ROPE_THETA ** (-jnp.arange(half, dtype=F32) / half)
    ang = pos.astype(F32)[:, None] * inv_freq
    c, s = jnp.cos(ang), jnp.sin(ang)
    z = jnp.zeros((pos.shape[0], LANE - QK_ROPE), F32)
    return jnp.concatenate([c, c, z], axis=1), jnp.concatenate([-s, s, z], axis=1)


def kernel(x_prompt, x_sample, cache_ckv, cache_krope, state_conv, page_table, c_prompt, c_sample,
           w_ada, b_ada, norm_mix_g, norm_ffn_g, w_in, conv_w, q_norm_g, w_uq, kv_norm_g, w_ukv, w_o,
           router_group, router_expert, w_gate, w_up, w_down, final_g):
    depth = w_ada.shape[0]
    bp, tp, d = x_prompt.shape
    bs, ts, _ = x_sample.shape
    assert depth == 1 and bp == 1 and ts == 1, "one layer, one prompt sequence, one new token per sample"
    n_pages = page_table.shape[1]
    page = cache_ckv.shape[2]
    d_conv = conv_w.shape[2]
    q_lora, kv_lora = q_norm_g.shape[1], kv_norm_g.shape[1]
    n_groups = router_group.shape[2]
    n_experts = router_expert.shape[2]
    assert n_experts + n_groups <= LANE and bs % SUBLANE == 0
    tm_p = 256
    assert tp % tm_p == 0 and (2 * tp) % ROUTE_BLOCK == 0

    c_all = jnp.concatenate([c_sample, c_prompt, jnp.zeros((SUBLANE - bp, d), F32)], axis=0)
    mod = _ada(c_all, w_ada[0], b_ada[0][None, :])
    prompt_mod_block = bs // SUBLANE

    win = jnp.pad(w_in[0].astype(BF16), ((0, 0), (0, LANE - QK_ROPE)))
    wuq3 = w_uq[0].astype(BF16).reshape(q_lora, N_HEADS, QK_NOPE + QK_ROPE)
    wuq = jnp.concatenate(
        [wuq3[:, :, :QK_NOPE].reshape(q_lora, N_HEADS * QK_NOPE),
         jnp.pad(wuq3[:, :, QK_NOPE:], ((0, 0), (0, 0), (0, ROPE_PAD - QK_ROPE))).reshape(
             q_lora, N_HEADS * ROPE_PAD)], axis=1)
    wukv3 = w_ukv[0].astype(BF16).reshape(kv_lora, N_HEADS, QK_NOPE + V_HEAD)
    wukv = jnp.concatenate(
        [wukv3[:, :, :QK_NOPE].reshape(kv_lora, N_HEADS * QK_NOPE),
         wukv3[:, :, QK_NOPE:].reshape(kv_lora, N_HEADS * V_HEAD)], axis=1)
    wo = w_o[0].astype(BF16)
    router = jnp.pad(jnp.concatenate([router_expert[0], router_group[0]], axis=1).astype(BF16),
                     ((0, 0), (0, LANE - n_experts - n_groups)))
    g_mix, g_ffn = norm_mix_g[0][None, :], norm_ffn_g[0][None, :]
    q_g, kv_g = q_norm_g[0][None, :], kv_norm_g[0][None, :]

    cos_p, sin_p = _rope_tables(jnp.arange(tp, dtype=I32))
    xp = x_prompt.reshape(tp, d)
    yconv_p, q_p, k_p, v_p, ckv_p, kr_p, cst_p = _inproj(
        xp, mod, prompt_mod_block, g_mix, win, conv_w[0], None, q_g, kv_g, wuq, wukv, cos_p, sin_p,
        sample=False, tm=tm_p)
    yattn_p = _flash(q_p, k_p, v_p, tq=512)

    past = n_pages * page
    cos_s, sin_s = _rope_tables(jnp.full((bs,), past, I32))
    xs_tok = x_sample.reshape(bs, d)
    yconv_s, qn_s, qr_s, ckv_s, kr_s, cn0_s, cn1_s = _inproj(
        xs_tok, mod, 0, g_mix, win, conv_w[0], (state_conv[0, :, 0], state_conv[0, :, 1]), q_g,
        kv_g, wuq, None, cos_s, sin_s, sample=True, tm=bs)
    q_lat = _qlat(qn_s, w_ukv[0]).reshape(bs, N_HEADS, kv_lora)
    q_rope = qr_s.reshape(bs, N_HEADS, ROPE_PAD)[:, :, :QK_ROPE]
    o_lat = _decode(page_table, q_lat, q_rope, ckv_s[:, None, :], kr_s[:, None, :], cache_ckv[0],
                    cache_krope[0], pages=16)
    yattn_s = _uv(o_lat.reshape(bs, N_HEADS * kv_lora), w_ukv[0])

    zeros_cnt = jnp.zeros((SUBLANE, LANE), F32)
    x1_p, h2_p, info_p, cnt_p = _oproj(
        xp, yconv_p, yattn_p, mod, prompt_mod_block, g_ffn, wo, router, zeros_cnt,
        sample=False, tm=tm_p, n_experts=n_experts, n_groups=n_groups)
    x1_s, h2_s, info_s, cnt_s = _oproj(
        xs_tok, yconv_s, yattn_s, mod, 0, g_ffn, wo, router, cnt_p,
        sample=True, tm=bs, n_experts=n_experts, n_groups=n_groups)

    n_tok = tp + bs
    info = jnp.concatenate([info_p[:, :4], info_s[:, :4]], axis=0).astype(I32)
    counts = cnt_s[0, :n_experts].astype(I32)
    padded = (counts + ROUTE_BLOCK - 1) // ROUTE_BLOCK * ROUTE_BLOCK
    pend = jnp.cumsum(padded)
    pstart = pend - padded
    dest = (pstart[info[:, 0:2]] + info[:, 2:4]).reshape(-1)
    n_blocks = -(-(n_tok * 2) // ROUTE_BLOCK) + n_experts
    n_slots = n_blocks * ROUTE_BLOCK
    nused = (pend[-1] // ROUTE_BLOCK).astype(I32)
    blk = jnp.minimum(jnp.arange(n_blocks, dtype=I32), nused - 1) * ROUTE_BLOCK
    blk_e = jnp.minimum(jnp.searchsorted(pend, blk, side='right'), n_experts - 1).astype(I32)
    fill = jnp.stack([jnp.append(pstart + counts, pend[-1]),
                      jnp.append(padded - counts, n_slots - pend[-1])], axis=1).reshape(-1).astype(I32)

    xs = _scatter(dest, fill, h2_p, h2_s, n_slots, tm=tm_p)
    yb = _ffn(blk_e, nused.reshape(1), xs, w_gate[0], w_up[0], w_down[0])

    fg = final_g[None, :]
    y_p = _final(dest, x1_p, info_p, mod, prompt_mod_block, fg, yb, tm=tm_p, base=0, per_row=False)
    y_s = _final(dest, x1_s, info_s, mod, 0, fg, yb, tm=bs, base=tp, per_row=True)

    conv_p = cst_p[SUBLANE - 2:, :]
    conv_s = jnp.stack([cn0_s, cn1_s], axis=1)
    return (y_p.reshape(bp, tp, d), y_s.reshape(bs, ts, d),
            ckv_p.reshape(depth, bp, tp, kv_lora), kr_p.reshape(depth, bp, tp, QK_ROPE),
            conv_p.reshape(depth, bp, 2, d_conv),
            ckv_s.reshape(depth, bs, ts, kv_lora), kr_s.reshape(depth, bs, ts, QK_ROPE),
            conv_s.reshape(depth, bs, 2, d_conv))
```

```python
import functools

import jax
import jax.numpy as jnp
from jax import lax
from jax.experimental import pallas as pl
from jax.experimental.pallas import tpu as pltpu

F32, BF16, I32 = jnp.float32, jnp.bfloat16, jnp.int32

N_HEADS = 8
QK_NOPE = 128
QK_ROPE = 64
V_HEAD = 128
ROPE_THETA = 10000.0
EPS = 1e-6
ATTN_SCALE = (QK_NOPE + QK_ROPE) ** -0.5
ROUTE_BLOCK = 256
N_MOD = 6

LANE = 128
SUBLANE = 8
VMEM_LIMIT = 56 * 1024 * 1024

ROPE_PAD = LANE
DQK = QK_NOPE + ROPE_PAD


def _params(*sem):
    return pltpu.CompilerParams(dimension_semantics=sem, vmem_limit_bytes=VMEM_LIMIT)


def _resident(shape):
    nd = len(shape)
    return pl.BlockSpec(shape, lambda *_: (0,) * nd, pipeline_mode=pl.Buffered(1))


def _dot(a, b):
    return jnp.dot(a, b, preferred_element_type=F32)


def _dot_t(a, b):
    return lax.dot_general(a, b, (((1,), (1,)), ((), ())), preferred_element_type=F32)


def _mm(a, w):
    return _dot(a.astype(BF16), w.astype(BF16))


def _rms(x, g):
    return x * lax.rsqrt(jnp.mean(x * x, axis=-1, keepdims=True) + EPS) * g


def _rope(x, cos, sin):
    n = x.shape[1] // LANE
    if n > 1:
        cos = jnp.concatenate([cos] * n, axis=1)
        sin = jnp.concatenate([sin] * n, axis=1)
    lane = lax.broadcasted_iota(I32, x.shape, 1)
    first_half = (lane % LANE) < (QK_ROPE // 2)
    partner = jnp.where(first_half,
                        pltpu.roll(x, x.shape[1] - QK_ROPE // 2, 1),
                        pltpu.roll(x, QK_ROPE // 2, 1))
    return x * cos + partner * sin


def _mod_row(ref, per_row):
    return ref[...] if per_row else ref[0:1, :]


def _ada_kernel(c_ref, w_ref, b_ref, o_ref):
    o_ref[...] = _mm(jax.nn.silu(c_ref[...]), w_ref[...]) + b_ref[...]


def _ada(c_all, w_ada, b_ada):
    m, d = c_all.shape
    n = w_ada.shape[1]
    tn = 512
    return pl.pallas_call(
        _ada_kernel,
        grid=(n // tn,),
        in_specs=[pl.BlockSpec((m, d), lambda j: (0, 0)),
                  pl.BlockSpec((d, tn), lambda j: (0, j)),
                  pl.BlockSpec((1, tn), lambda j: (0, j))],
        out_specs=pl.BlockSpec((m, tn), lambda j: (0, j)),
        out_shape=jax.ShapeDtypeStruct((m, n), F32),
        compiler_params=_params("arbitrary"),
        name="ada_mod",
    )(c_all, w_ada, b_ada)


def _inproj_kernel(*refs, sample, tm, d_conv, q_lora, kv_lora):
    it = iter(refs)
    x_ref, sh_ref, sc_ref, g_ref = next(it), next(it), next(it), next(it)
    win = next(it)
    cw_ref = next(it)
    if sample:
        st0_ref, st1_ref = next(it), next(it)
    qg_ref, kvg_ref = next(it), next(it)
    wuq = next(it)
    wukv = None if sample else next(it)
    cos_ref, sin_ref = next(it), next(it)
    if sample:
        yconv_ref, qn_ref, qr_ref, ckv_ref, kr_ref, cn0_ref, cn1_ref = it
    else:
        yconv_ref, q_ref, k_ref, v_ref, ckv_ref, kr_ref, cst_ref, carry_ref = it

    def proj(a, lo, hi_col):
        return _mm(a, win[:, lo:hi_col])

    x = x_ref[...]
    h = _rms(x, g_ref[...]) * (1.0 + _mod_row(sc_ref, sample)) + _mod_row(sh_ref, sample)

    c1, c2, c3 = d_conv, 2 * d_conv, 3 * d_conv
    u = proj(h, c2, c3) * proj(h, 0, c1)
    if sample:
        u1, u2 = st1_ref[...], st0_ref[...]
        cn0_ref[...] = u1
        cn1_ref[...] = u
    else:
        @pl.when(pl.program_id(0) == 0)
        def _():
            carry_ref[...] = jnp.zeros_like(carry_ref)
        prev = carry_ref[...]
        p1, p2 = prev[SUBLANE - 1:SUBLANE, :], prev[SUBLANE - 2:SUBLANE - 1, :]
        rows = lax.broadcasted_iota(I32, u.shape, 0)
        u1 = jnp.where(rows == 0, p1, pltpu.roll(u, 1, 0))
        u2 = jnp.where(rows == 0, p2, jnp.where(rows == 1, p1, pltpu.roll(u, 2, 0)))
        tail = u[tm - SUBLANE:tm, :]
        carry_ref[...] = tail
        cst_ref[...] = tail
    conv = cw_ref[0:1, :] * u2 + cw_ref[1:2, :] * u1 + cw_ref[2:3, :] * u
    yconv_ref[...] = (proj(h, c1, c2) * conv).astype(yconv_ref.dtype)

    cos, sin = cos_ref[...], sin_ref[...]
    c4, c5 = c3 + q_lora, c3 + q_lora + kv_lora
    cqn = _rms(proj(h, c3, c4), qg_ref[...])
    ckv = _rms(proj(h, c4, c5), kvg_ref[...])
    kr = _rope(proj(h, c5, c5 + LANE), cos, sin)
    ckv_ref[...] = ckv
    kr_ref[...] = kr[:, :QK_ROPE]
    hq = N_HEADS * QK_NOPE
    q = _mm(cqn, wuq[...])
    if sample:
        qn_ref[...] = q[:, :hq]
        qr_ref[...] = _rope(q[:, hq:], cos, sin)
    else:
        q = q * ATTN_SCALE
        qn = q[:, :hq]
        qr = _rope(q[:, hq:], cos, sin)
        kv = _mm(ckv, wukv[...])
        krb = kr.astype(BF16)
        for hd in range(N_HEADS):
            lo, hi = hd * LANE, (hd + 1) * LANE
            q_ref[hd, :, 0:QK_NOPE] = qn[:, lo:hi].astype(BF16)
            q_ref[hd, :, QK_NOPE:DQK] = qr[:, lo:hi].astype(BF16)
            k_ref[hd, :, 0:QK_NOPE] = kv[:, lo:hi].astype(BF16)
            k_ref[hd, :, QK_NOPE:DQK] = krb
            v_ref[hd] = kv[:, hq + lo:hq + hi].astype(BF16)


def _inproj(x, mod, mod_row_block, norm_g, win, conv_w, state, q_g, kv_g, wuq, wukv, cos, sin, *,
            sample, tm):
    n, d = x.shape
    d_conv = conv_w.shape[1]
    q_lora, kv_lora = q_g.shape[1], kv_g.shape[1]
    rows = tm if sample else SUBLANE
    hq = N_HEADS * QK_NOPE

    def tile(width):
        return pl.BlockSpec((tm, width), lambda i: (i, 0))

    def modspec(col):
        return pl.BlockSpec((rows, d), lambda i: (mod_row_block, col))

    args = [x, mod, mod, norm_g, win, conv_w]
    specs = [tile(d), modspec(0), modspec(1), _resident(norm_g.shape), _resident(win.shape),
             _resident(conv_w.shape)]
    if sample:
        args += list(state)
        specs += [tile(d_conv), tile(d_conv)]
    args += [q_g, kv_g, wuq]
    specs += [_resident(q_g.shape), _resident(kv_g.shape), _resident(wuq.shape)]
    if not sample:
        args.append(wukv)
        specs.append(_resident(wukv.shape))
    args += [cos, sin]
    specs += [tile(LANE), tile(LANE)]

    if sample:
        out_shape = [jax.ShapeDtypeStruct((n, d_conv), F32), jax.ShapeDtypeStruct((n, hq), F32),
                     jax.ShapeDtypeStruct((n, N_HEADS * ROPE_PAD), F32),
                     jax.ShapeDtypeStruct((n, kv_lora), F32), jax.ShapeDtypeStruct((n, QK_ROPE), F32),
                     jax.ShapeDtypeStruct((n, d_conv), F32), jax.ShapeDtypeStruct((n, d_conv), F32)]
        out_specs = [tile(d_conv), tile(hq), tile(N_HEADS * ROPE_PAD), tile(kv_lora), tile(QK_ROPE),
                     tile(d_conv), tile(d_conv)]
        scratch = []
    else:
        def heads(width):
            return pl.BlockSpec((N_HEADS, tm, width), lambda i: (0, i, 0))
        out_shape = [jax.ShapeDtypeStruct((n, d_conv), BF16),
                     jax.ShapeDtypeStruct((N_HEADS, n, DQK), BF16),
                     jax.ShapeDtypeStruct((N_HEADS, n, DQK), BF16),
                     jax.ShapeDtypeStruct((N_HEADS, n, V_HEAD), BF16),
                     jax.ShapeDtypeStruct((n, kv_lora), F32), jax.ShapeDtypeStruct((n, QK_ROPE), F32),
                     jax.ShapeDtypeStruct((SUBLANE, d_conv), F32)]
        out_specs = [tile(d_conv), heads(DQK), heads(DQK), heads(V_HEAD), tile(kv_lora), tile(QK_ROPE),
                     pl.BlockSpec((SUBLANE, d_conv), lambda i: (0, 0))]
        scratch = [pltpu.VMEM((SUBLANE, d_conv), F32)]

    return pl.pallas_call(
        functools.partial(_inproj_kernel, sample=sample, tm=tm, d_conv=d_conv, q_lora=q_lora,
                          kv_lora=kv_lora),
        grid=(n // tm,),
        in_specs=specs, out_specs=out_specs, out_shape=out_shape, scratch_shapes=scratch,
        compiler_params=_params("arbitrary"),
        name="inproj_sample" if sample else "inproj_prompt",
    )(*args)


def _flash_kernel(q_ref, k_ref, v_ref, o_ref, m_sc, l_sc, acc_sc, *, tq, heads):
    qi = pl.program_id(1)
    m_sc[...] = jnp.full_like(m_sc, -jnp.inf)
    l_sc[...] = jnp.zeros_like(l_sc)
    acc_sc[...] = jnp.zeros_like(acc_sc)
    reps = tq // LANE

    def step(ki, masked):
        off = pl.multiple_of(ki * tq, tq)
        for hd in range(heads):
            s = _dot_t(q_ref[hd], k_ref[hd, pl.ds(off, tq), :])
            if masked:
                row = lax.broadcasted_iota(I32, s.shape, 0)
                col = lax.broadcasted_iota(I32, s.shape, 1)
                s = jnp.where(col <= row, s, -jnp.inf)
            m_prev = m_sc[hd]
            m_new = jnp.maximum(m_prev, jnp.max(s, axis=1, keepdims=True))
            alpha = jnp.exp(m_prev - m_new)
            p = jnp.exp(s - jnp.concatenate([m_new] * reps, axis=1))
            l_sc[hd] = alpha * l_sc[hd] + jnp.sum(p, axis=1, keepdims=True)
            acc_sc[hd] = alpha * acc_sc[hd] + _dot(p.astype(BF16), v_ref[hd, pl.ds(off, tq), :])
            m_sc[hd] = m_new

    def body(ki, carry):
        step(ki, False)
        return carry

    lax.fori_loop(0, qi, body, 0)
    step(qi, True)
    for hd in range(heads):
        o_ref[:, hd * V_HEAD:(hd + 1) * V_HEAD] = (acc_sc[hd] / l_sc[hd]).astype(o_ref.dtype)


def _flash(q, k, v, *, tq, heads):
    h, t, _ = q.shape
    assert V_HEAD == LANE and h % heads == 0 and t % tq == 0
    return pl.pallas_call(
        functools.partial(_flash_kernel, tq=tq, heads=heads),
        grid=(h // heads, t // tq),
        in_specs=[pl.BlockSpec((heads, tq, DQK), lambda hh, i: (hh, i, 0)),
                  pl.BlockSpec((heads, t, DQK), lambda hh, i: (hh, 0, 0), pipeline_mode=pl.Buffered(1)),
                  pl.BlockSpec((heads, t, V_HEAD), lambda hh, i: (hh, 0, 0),
                               pipeline_mode=pl.Buffered(1))],
        out_specs=pl.BlockSpec((tq, heads * V_HEAD), lambda hh, i: (i, hh)),
        out_shape=jax.ShapeDtypeStruct((t, h * V_HEAD), BF16),
        scratch_shapes=[pltpu.VMEM((heads, tq, LANE), F32), pltpu.VMEM((heads, tq, LANE), F32),
                        pltpu.VMEM((heads, tq, V_HEAD), F32)],
        compiler_params=_params("arbitrary", "arbitrary"),
        name="flash_prompt",
    )(q, k, v)


def _qlat_kernel(qn_ref, w_ref, o_ref):
    o_ref[...] = _dot_t(qn_ref[...].astype(BF16), w_ref[:, :QK_NOPE].astype(BF16))


def _qlat(qn, w_ukv):
    n = qn.shape[0]
    kv_lora = w_ukv.shape[0]
    per_head = QK_NOPE + V_HEAD
    return pl.pallas_call(
        _qlat_kernel,
        grid=(N_HEADS,),
        in_specs=[pl.BlockSpec((n, QK_NOPE), lambda h: (0, h)),
                  pl.BlockSpec((kv_lora, per_head), lambda h: (0, h))],
        out_specs=pl.BlockSpec((n, kv_lora), lambda h: (0, h)),
        out_shape=jax.ShapeDtypeStruct((n, N_HEADS * kv_lora), F32),
        compiler_params=_params("arbitrary"),
        name="sample_qlat",
    )(qn, w_ukv)


def _uv_kernel(o_ref_in, w_ref, y_ref):
    y_ref[...] = _mm(o_ref_in[...], w_ref[:, QK_NOPE:])


def _uv(o_lat, w_ukv):
    n = o_lat.shape[0]
    kv_lora = w_ukv.shape[0]
    per_head = QK_NOPE + V_HEAD
    return pl.pallas_call(
        _uv_kernel,
        grid=(N_HEADS,),
        in_specs=[pl.BlockSpec((n, kv_lora), lambda h: (0, h)),
                  pl.BlockSpec((kv_lora, per_head), lambda h: (0, h))],
        out_specs=pl.BlockSpec((n, V_HEAD), lambda h: (0, h)),
        out_shape=jax.ShapeDtypeStruct((n, N_HEADS * V_HEAD), F32),
        compiler_params=_params("arbitrary"),
        name="sample_uv",
    )(o_lat, w_ukv)


def _decode_kernel(pt_ref, ql_ref, qr_ref, cn_ref, kn_ref, ckv_hbm, kr_hbm, o_ref,
                   ckv_buf, kr_buf, sem, m_sc, l_sc, acc_sc, *, pages, page):
    s_idx, j = pl.program_id(0), pl.program_id(1)
    n_seq, nj = pl.num_programs(0), pl.num_programs(1)
    step = s_idx * nj + j

    def chunk_copies(seq, chunk, slot):
        out = []
        for p in range(pages):
            pid = pt_ref[seq, chunk * pages + p]
            rows = pl.ds(p * page, page)
            out.append(pltpu.make_async_copy(ckv_hbm.at[pid], ckv_buf.at[slot, rows], sem.at[slot, 0]))
            out.append(pltpu.make_async_copy(kr_hbm.at[pid], kr_buf.at[slot, :, rows], sem.at[slot, 1]))
        return out

    @pl.when(step == 0)
    def _():
        for c in chunk_copies(0, 0, 0):
            c.start()

    @pl.when(step + 1 < n_seq * nj)
    def _():
        wrap = j + 1 == nj
        for c in chunk_copies(jnp.where(wrap, s_idx + 1, s_idx), jnp.where(wrap, 0, j + 1),
                              (step + 1) % 2):
            c.start()

    slot = step % 2
    for c in chunk_copies(s_idx, j, slot):
        c.wait()

    ql = ql_ref[0].astype(BF16).astype(F32)
    qr = qr_ref[0].astype(BF16).astype(F32)

    @pl.when(j == 0)
    def _():
        m_sc[...] = jnp.full_like(m_sc, -jnp.inf)
        l_sc[...] = jnp.zeros_like(l_sc)
        acc_sc[...] = jnp.zeros_like(acc_sc)

    ckv = ckv_buf[slot]
    s = (_dot_t(ql, ckv) + _dot(qr, kr_buf[slot])) * ATTN_SCALE
    m_prev = m_sc[...]
    m_new = jnp.maximum(m_prev, jnp.max(s, axis=1, keepdims=True))
    alpha = jnp.exp(m_prev - m_new)
    pr = jnp.exp(s - m_new)
    l_new = alpha * l_sc[...] + jnp.sum(pr, axis=1, keepdims=True)
    acc = alpha * acc_sc[...] + _dot(pr, ckv)
    m_sc[...] = m_new
    l_sc[...] = l_new
    acc_sc[...] = acc

    @pl.when(j == pl.num_programs(1) - 1)
    def _():
        cn = cn_ref[0].astype(BF16).astype(F32)
        kn = kn_ref[0].astype(BF16).astype(F32)
        s_new = (jnp.sum(ql * cn, axis=1, keepdims=True)
                 + jnp.sum(qr * kn, axis=1, keepdims=True)) * ATTN_SCALE
        m_fin = jnp.maximum(m_new, s_new)
        a_fin = jnp.exp(m_new - m_fin)
        p_new = jnp.exp(s_new - m_fin)
        l_fin = a_fin * l_new + p_new
        o_ref[0] = (a_fin * acc + p_new * cn) / l_fin


def _decode(page_table, q_lat, q_rope, ckv_new, kr_new, cache_ckv, cache_kr_t, *, pages):
    b, n_pages = page_table.shape
    h, kv_lora = q_lat.shape[1:]
    rope = q_rope.shape[2]
    page = cache_ckv.shape[1]

    assert n_pages % pages == 0
    in_specs = [pl.BlockSpec((1, h, kv_lora), lambda s, j, pt: (s, 0, 0)),
                pl.BlockSpec((1, h, rope), lambda s, j, pt: (s, 0, 0)),
                pl.BlockSpec((1, 1, kv_lora), lambda s, j, pt: (s, 0, 0)),
                pl.BlockSpec((1, 1, rope), lambda s, j, pt: (s, 0, 0)),
                pl.BlockSpec(memory_space=pl.ANY),
                pl.BlockSpec(memory_space=pl.ANY)]
    grid_spec = pltpu.PrefetchScalarGridSpec(
        num_scalar_prefetch=1,
        grid=(b, n_pages // pages),
        in_specs=in_specs,
        out_specs=pl.BlockSpec((1, h, kv_lora), lambda s, j, pt: (s, 0, 0)),
        scratch_shapes=[pltpu.VMEM((2, pages * page, kv_lora), F32),
                        pltpu.VMEM((2, rope, pages * page), F32),
                        pltpu.SemaphoreType.DMA((2, 2)),
                        pltpu.VMEM((h, 1), F32), pltpu.VMEM((h, 1), F32),
                        pltpu.VMEM((h, kv_lora), F32)])
    return pl.pallas_call(
        functools.partial(_decode_kernel, pages=pages, page=page),
        grid_spec=grid_spec,
        out_shape=jax.ShapeDtypeStruct((b, h, kv_lora), F32),
        compiler_params=_params("arbitrary", "arbitrary"),
        name="decode_sample",
    )(page_table, q_lat, q_rope, ckv_new, kr_new, cache_ckv, cache_kr_t)


def _oproj_kernel(*refs, sample, tm, d_conv, n_experts, n_groups):
    it = iter(refs)
    x_ref, yc_ref, ya_ref, g1_ref, sh_ref, sc_ref, ng_ref = (next(it) for _ in range(7))
    wo, router, cnt_in_ref = next(it), next(it), next(it)
    x1_ref, h2_ref, info_ref, cnt_ref, carry_ref = it

    d_mix = wo.shape[0]
    o = _mm(yc_ref[...], wo[0:d_conv, :]) + _mm(ya_ref[...], wo[d_conv:d_mix, :])
    x1 = x_ref[...] + _mod_row(g1_ref, sample) * o
    x1_ref[...] = x1
    h2 = _rms(x1, ng_ref[...]) * (1.0 + _mod_row(sc_ref, sample)) + _mod_row(sh_ref, sample)
    h2_ref[...] = h2

    lg = _mm(h2, router[...])
    lane_i = lax.broadcasted_iota(I32, lg.shape, 1)
    lane = lane_i.astype(F32)
    big = float(LANE)
    neg = -jnp.inf
    gmask = (lane_i >= n_experts) & (lane_i < n_experts + n_groups)
    gl = jnp.where(gmask, lg, neg)
    gmax = jnp.max(gl, axis=1, keepdims=True)
    g_sel = jnp.min(jnp.where(gl == gmax, lane, big), axis=1, keepdims=True) - float(n_experts)
    g_w = 1.0 / jnp.sum(jnp.where(gmask, jnp.exp(lg - gmax), 0.0), axis=1, keepdims=True)
    per_group = n_experts // n_groups
    emask = (lane_i < n_experts) & ((lane_i // per_group).astype(F32) == g_sel)
    el = jnp.where(emask, lg, neg)
    v1 = jnp.max(el, axis=1, keepdims=True)
    i1 = jnp.min(jnp.where(el == v1, lane, big), axis=1, keepdims=True)
    el2 = jnp.where(lane == i1, neg, el)
    v2 = jnp.max(el2, axis=1, keepdims=True)
    i2 = jnp.min(jnp.where(el2 == v2, lane, big), axis=1, keepdims=True)
    e2 = jnp.exp(v2 - v1)
    w1 = (1.0 / (1.0 + e2)) * g_w
    w2 = (e2 / (1.0 + e2)) * g_w

    @pl.when(pl.program_id(0) == 0)
    def _():
        carry_ref[...] = cnt_in_ref[...]
    hit1, hit2 = lane == i1, lane == i2
    chosen = jnp.where(hit1 | hit2, 1.0, 0.0)
    row = lax.broadcasted_iota(I32, (tm, tm), 0)
    col = lax.broadcasted_iota(I32, (tm, tm), 1)
    before = jnp.where(col < row, 1.0, 0.0).astype(BF16)
    running = _dot(before, chosen.astype(BF16)) + carry_ref[0:1, :]
    rank1 = jnp.sum(jnp.where(hit1, running, 0.0), axis=1, keepdims=True)
    rank2 = jnp.sum(jnp.where(hit2, running, 0.0), axis=1, keepdims=True)
    total = carry_ref[...] + jnp.sum(chosen, axis=0, keepdims=True)
    carry_ref[...] = total
    cnt_ref[...] = total

    info = jnp.where(lane_i == 0, i1, 0.0)
    info = jnp.where(lane_i == 1, i2, info)
    info = jnp.where(lane_i == 2, rank1, info)
    info = jnp.where(lane_i == 3, rank2, info)
    info = jnp.where(lane_i == 4, w1, info)
    info = jnp.where(lane_i == 5, w2, info)
    info_ref[...] = info


def _oproj(x, yconv, yattn, mod, mod_row_block, norm_g, wo, router, cnt_in, *, sample, tm,
           n_experts, n_groups):
    n, d = x.shape
    d_conv = yconv.shape[1]
    rows = tm if sample else SUBLANE

    def tile(width):
        return pl.BlockSpec((tm, width), lambda i: (i, 0))

    def modspec(col):
        return pl.BlockSpec((rows, d), lambda i: (mod_row_block, col))

    args = [x, yconv, yattn, mod, mod, mod, norm_g]
    specs = [tile(d), tile(d_conv), tile(yattn.shape[1]), modspec(2), modspec(3), modspec(4),
             _resident(norm_g.shape)]
    for w in (wo, router, cnt_in):
        args.append(w)
        specs.append(_resident(w.shape))
    return pl.pallas_call(
        functools.partial(_oproj_kernel, sample=sample, tm=tm, d_conv=d_conv, n_experts=n_experts,
                          n_groups=n_groups),
        grid=(n // tm,),
        in_specs=specs,
        out_specs=[tile(d), tile(d), tile(LANE), pl.BlockSpec((SUBLANE, LANE), lambda i: (0, 0))],
        out_shape=[jax.ShapeDtypeStruct((n, d), F32), jax.ShapeDtypeStruct((n, d), F32),
                   jax.ShapeDtypeStruct((n, LANE), F32), jax.ShapeDtypeStruct((SUBLANE, LANE), F32)],
        scratch_shapes=[pltpu.VMEM((SUBLANE, LANE), F32)],
        compiler_params=_params("arbitrary"),
        name="oproj_sample" if sample else "oproj_prompt",
    )(*args)


def _row_copy(src, src_row, dst, dst_row, sem):
    return pltpu.make_async_copy(src.at[pl.ds(src_row, 1)], dst.at[pl.ds(dst_row, 1)], sem)


def _scatter_kernel(dest_ref, fill_ref, hp_ref, hs_ref, xs_ref, zero_ref, sem, *, tm, n_prompt_tiles):
    i = pl.program_id(0)

    def scatter_rows(h_ref, n_rows, base):
        def start(r, carry):
            t = base + r
            _row_copy(h_ref, r, xs_ref, dest_ref[2 * t], sem).start()
            _row_copy(h_ref, r, xs_ref, dest_ref[2 * t + 1], sem).start()
            return carry

        lax.fori_loop(0, n_rows, start, 0, unroll=8)
        all_rows = xs_ref.at[pl.ds(0, 2 * n_rows)]
        pltpu.make_async_copy(all_rows, all_rows, sem).wait()

    @pl.when(i < n_prompt_tiles)
    def _():
        scatter_rows(hp_ref, tm, i * tm)

    @pl.when(i == n_prompt_tiles)
    def _():
        scatter_rows(hs_ref, hs_ref.shape[0], n_prompt_tiles * tm)

    @pl.when(i == 0)
    def _():
        zero_ref[...] = jnp.zeros_like(zero_ref)

        def each_range(act):
            def per_range(e, carry):
                first, count = fill_ref[2 * e], fill_ref[2 * e + 1]
                lax.fori_loop(0, count, lambda r, c: act(first + r, c), 0)
                return carry
            lax.fori_loop(0, fill_ref.shape[0] // 2, per_range, 0)

        def zstart(slot, c):
            _row_copy(zero_ref, 0, xs_ref, slot, sem).start()
            return c

        def zwait(slot, c):
            _row_copy(zero_ref, 0, xs_ref, 0, sem).wait()
            return c

        each_range(zstart)
        each_range(zwait)


def _scatter(dest, fill, h2_p, h2_s, n_slots, *, tm):
    n_p, d = h2_p.shape
    n_prompt_tiles = n_p // tm
    grid_spec = pltpu.PrefetchScalarGridSpec(
        num_scalar_prefetch=2,
        grid=(n_prompt_tiles + 1,),
        in_specs=[pl.BlockSpec((tm, d), lambda i, *_: (jnp.minimum(i, n_prompt_tiles - 1), 0)),
                  pl.BlockSpec(h2_s.shape, lambda i, *_: (0, 0))],
        out_specs=pl.BlockSpec(memory_space=pl.ANY),
        scratch_shapes=[pltpu.VMEM((SUBLANE, d), F32), pltpu.SemaphoreType.DMA(())])
    return pl.pallas_call(
        functools.partial(_scatter_kernel, tm=tm, n_prompt_tiles=n_prompt_tiles),
        grid_spec=grid_spec,
        out_shape=jax.ShapeDtypeStruct((n_slots, d), F32),
        compiler_params=_params("arbitrary"),
        name="moe_scatter",
    )(dest, fill, h2_p, h2_s)


def _ffn_kernel(blk_e_ref, nused_ref, xs_ref, wg_ref, wu_ref, wd_ref, o_ref):
    del blk_e_ref

    used = pl.program_id(0) < nused_ref[0]

    @pl.when(used)
    def _():
        x = xs_ref[...]
        hdn = jax.nn.silu(_dot(x, wg_ref[0])) * _dot(x, wu_ref[0])
        o_ref[...] = _dot(hdn, wd_ref[0])

    @pl.when(jnp.logical_not(used))
    def _():
        o_ref[...] = jnp.zeros_like(o_ref)


def _ffn(blk_e, nused, xs, w_gate, w_up, w_down):
    n_slots, d = xs.shape
    n_blocks = n_slots // ROUTE_BLOCK
    d_e = w_gate.shape[2]

    def rows(b, be, nu):
        return (jnp.minimum(b, nu[0] - 1), 0)

    grid_spec = pltpu.PrefetchScalarGridSpec(
        num_scalar_prefetch=2,
        grid=(n_blocks,),
        in_specs=[pl.BlockSpec((ROUTE_BLOCK, d), rows),
                  pl.BlockSpec((1, d, d_e), lambda b, be, nu: (be[b], 0, 0)),
                  pl.BlockSpec((1, d, d_e), lambda b, be, nu: (be[b], 0, 0)),
                  pl.BlockSpec((1, d_e, d), lambda b, be, nu: (be[b], 0, 0))],
        out_specs=pl.BlockSpec((ROUTE_BLOCK, d), lambda b, be, nu: (b, 0)))
    return pl.pallas_call(
        _ffn_kernel,
        grid_spec=grid_spec,
        out_shape=jax.ShapeDtypeStruct((n_slots, d), F32),
        compiler_params=_params("arbitrary"),
        name="moe_experts",
    )(blk_e, nused, xs, w_gate, w_up, w_down)


def _final_kernel(dest_ref, x1_ref, info_ref, g2_ref, fg_ref, yb_ref, o_ref, rows_ref, sem, *,
                  tm, base, per_row):
    i = pl.program_id(0)
    n = pl.num_programs(0)

    def issue(tile_idx, slot):
        def start(r, carry):
            t = base + tile_idx * tm + r
            _row_copy(yb_ref, dest_ref[2 * t], rows_ref.at[slot], r, sem.at[slot]).start()
            _row_copy(yb_ref, dest_ref[2 * t + 1], rows_ref.at[slot], tm + r, sem.at[slot]).start()
            return carry
        lax.fori_loop(0, tm, start, 0, unroll=8)

    @pl.when(i == 0)
    def _():
        issue(0, 0)

    @pl.when(i + 1 < n)
    def _():
        issue(i + 1, (i + 1) % 2)

    slot = i % 2

    pltpu.make_async_copy(yb_ref.at[pl.ds(0, 2 * tm)], rows_ref.at[slot], sem.at[slot]).wait()
    info = info_ref[...]
    moe = info[:, 4:5] * rows_ref[slot, 0:tm] + info[:, 5:6] * rows_ref[slot, tm:2 * tm]
    y = x1_ref[...] + _mod_row(g2_ref, per_row) * moe
    o_ref[...] = _rms(y, fg_ref[...])


def _final(dest, x1, info, mod, mod_row_block, final_g, yb, *, tm, base, per_row):
    n, d = x1.shape
    rows = tm if per_row else SUBLANE
    grid_spec = pltpu.PrefetchScalarGridSpec(
        num_scalar_prefetch=1,
        grid=(n // tm,),
        in_specs=[pl.BlockSpec((tm, d), lambda i, *_: (i, 0)),
                  pl.BlockSpec((tm, LANE), lambda i, *_: (i, 0)),
                  pl.BlockSpec((rows, d), lambda i, *_: (mod_row_block, N_MOD - 1)),
                  pl.BlockSpec((1, d), lambda i, *_: (0, 0)),
                  pl.BlockSpec(memory_space=pl.ANY)],
        out_specs=pl.BlockSpec((tm, d), lambda i, *_: (i, 0)),
        scratch_shapes=[pltpu.VMEM((2, 2 * tm, d), F32), pltpu.SemaphoreType.DMA((2,))])
    return pl.pallas_call(
        functools.partial(_final_kernel, tm=tm, base=base, per_row=per_row),
        grid_spec=grid_spec,
        out_shape=jax.ShapeDtypeStruct((n, d), F32),
        compiler_params=_params("arbitrary"),
        name="moe_combine_final",
    )(dest, x1, info, mod, final_g, yb)


def _rope_tables(pos):
    half = QK_ROPE // 2
    inv_freq = ROPE_THETA ** (-jnp.arange(half, dtype=F32) / half)
    ang = pos.astype(F32)[:, None] * inv_freq
    c, s = jnp.cos(ang), jnp.sin(ang)
    z = jnp.zeros((pos.shape[0], LANE - QK_ROPE), F32)
    return jnp.concatenate([c, c, z], axis=1), jnp.concatenate([-s, s, z], axis=1)


def kernel(x_prompt, x_sample, cache_ckv, cache_krope, state_conv, page_table, c_prompt, c_sample,
           w_ada, b_ada, norm_mix_g, norm_ffn_g, w_in, conv_w, q_norm_g, w_uq, kv_norm_g, w_ukv, w_o,
           router_group, router_expert, w_gate, w_up, w_down, final_g):
    depth = w_ada.shape[0]
    bp, tp, d = x_prompt.shape
    bs, ts, _ = x_sample.shape
    assert depth == 1 and bp == 1 and ts == 1, "one layer, one prompt sequence, one new token per sample"
    n_pages = page_table.shape[1]
    page = cache_ckv.shape[2]
    d_conv = conv_w.shape[2]
    q_lora, kv_lora = q_norm_g.shape[1], kv_norm_g.shape[1]
    n_groups = router_group.shape[2]
    n_experts = router_expert.shape[2]
    assert n_experts + n_groups <= LANE and bs % SUBLANE == 0
    tm_p = 256
    assert tp % tm_p == 0 and (2 * tp) % ROUTE_BLOCK == 0

    c_all = jnp.concatenate([c_sample, c_prompt, jnp.zeros((SUBLANE - bp, d), F32)], axis=0)
    mod = _ada(c_all, w_ada[0], b_ada[0][None, :])
    prompt_mod_block = bs // SUBLANE

    win = jnp.pad(w_in[0].astype(BF16), ((0, 0), (0, LANE - QK_ROPE)))
    wuq3 = w_uq[0].astype(BF16).reshape(q_lora, N_HEADS, QK_NOPE + QK_ROPE)
    wuq = jnp.concatenate(
        [wuq3[:, :, :QK_NOPE].reshape(q_lora, N_HEADS * QK_NOPE),
         jnp.pad(wuq3[:, :, QK_NOPE:], ((0, 0), (0, 0), (0, ROPE_PAD - QK_ROPE))).reshape(
             q_lora, N_HEADS * ROPE_PAD)], axis=1)
    wukv3 = w_ukv[0].astype(BF16).reshape(kv_lora, N_HEADS, QK_NOPE + V_HEAD)
    wukv = jnp.concatenate(
        [wukv3[:, :, :QK_NOPE].reshape(kv_lora, N_HEADS * QK_NOPE),
         wukv3[:, :, QK_NOPE:].reshape(kv_lora, N_HEADS * V_HEAD)], axis=1)
    wo = w_o[0].astype(BF16)
    router = jnp.pad(jnp.concatenate([router_expert[0], router_group[0]], axis=1).astype(BF16),
                     ((0, 0), (0, LANE - n_experts - n_groups)))
    g_mix, g_ffn = norm_mix_g[0][None, :], norm_ffn_g[0][None, :]
    q_g, kv_g = q_norm_g[0][None, :], kv_norm_g[0][None, :]

    cos_p, sin_p = _rope_tables(jnp.arange(tp, dtype=I32))
    xp = x_prompt.reshape(tp, d)
    yconv_p, q_p, k_p, v_p, ckv_p, kr_p, cst_p = _inproj(
        xp, mod, prompt_mod_block, g_mix, win, conv_w[0], None, q_g, kv_g, wuq, wukv, cos_p, sin_p,
        sample=False, tm=tm_p)
    yattn_p = _flash(q_p, k_p, v_p, tq=512, heads=4)

    past = n_pages * page
    cos_s, sin_s = _rope_tables(jnp.full((bs,), past, I32))
    xs_tok = x_sample.reshape(bs, d)
    yconv_s, qn_s, qr_s, ckv_s, kr_s, cn0_s, cn1_s = _inproj(
        xs_tok, mod, 0, g_mix, win, conv_w[0], (state_conv[0, :, 0], state_conv[0, :, 1]), q_g,
        kv_g, wuq, None, cos_s, sin_s, sample=True, tm=bs)
    q_lat = _qlat(qn_s, w_ukv[0]).reshape(bs, N_HEADS, kv_lora)
    q_rope = qr_s.reshape(bs, N_HEADS, ROPE_PAD)[:, :, :QK_ROPE]
    cache_kr_t = jnp.swapaxes(cache_krope[0], 1, 2)
    o_lat = _decode(page_table, q_lat, q_rope, ckv_s[:, None, :], kr_s[:, None, :], cache_ckv[0],
                    cache_kr_t, pages=32)
    yattn_s = _uv(o_lat.reshape(bs, N_HEADS * kv_lora), w_ukv[0])

    zeros_cnt = jnp.zeros((SUBLANE, LANE), F32)
    x1_p, h2_p, info_p, cnt_p = _oproj(
        xp, yconv_p, yattn_p, mod, prompt_mod_block, g_ffn, wo, router, zeros_cnt,
        sample=False, tm=tm_p, n_experts=n_experts, n_groups=n_groups)
    x1_s, h2_s, info_s, cnt_s = _oproj(
        xs_tok, yconv_s, yattn_s, mod, 0, g_ffn, wo, router, cnt_p,
        sample=True, tm=bs, n_experts=n_experts, n_groups=n_groups)

    n_tok = tp + bs
    info = jnp.concatenate([info_p[:, :4], info_s[:, :4]], axis=0).astype(I32)
    counts = cnt_s[0, :n_experts].astype(I32)
    padded = (counts + ROUTE_BLOCK - 1) // ROUTE_BLOCK * ROUTE_BLOCK
    pend = jnp.cumsum(padded)
    pstart = pend - padded
    experts = jnp.arange(n_experts, dtype=I32)
    first_slot = jnp.sum(jnp.where(info[:, 0:2, None] == experts, pstart, 0), axis=-1)
    dest = (first_slot + info[:, 2:4]).reshape(-1)
    n_blocks = -(-(n_tok * 2) // ROUTE_BLOCK) + n_experts
    n_slots = n_blocks * ROUTE_BLOCK
    nused = (pend[-1] // ROUTE_BLOCK).astype(I32)
    blk = jnp.minimum(jnp.arange(n_blocks, dtype=I32), nused - 1) * ROUTE_BLOCK
    blk_e = jnp.minimum(jnp.sum((pend[None, :] <= blk[:, None]).astype(I32), axis=1), n_experts - 1)
    fill = jnp.stack([jnp.append(pstart + counts, pend[-1]),
                      jnp.append(padded - counts, n_slots - pend[-1])], axis=1).reshape(-1).astype(I32)

    xs = _scatter(dest, fill, h2_p, h2_s, n_slots, tm=tm_p)
    yb = _ffn(blk_e, nused.reshape(1), xs, w_gate[0], w_up[0], w_down[0])

    fg = final_g[None, :]
    y_p = _final(dest, x1_p, info_p, mod, prompt_mod_block, fg, yb, tm=tm_p, base=0, per_row=False)
    y_s = _final(dest, x1_s, info_s, mod, 0, fg, yb, tm=bs, base=tp, per_row=True)

    conv_p = cst_p[SUBLANE - 2:, :]
    conv_s = jnp.stack([cn0_s, cn1_s], axis=1)
    return (y_p.reshape(bp, tp, d), y_s.reshape(bs, ts, d),
            ckv_p.reshape(depth, bp, tp, kv_lora), kr_p.reshape(depth, bp, tp, QK_ROPE),
            conv_p.reshape(depth, bp, 2, d_conv),
            ckv_s.reshape(depth, bs, ts, kv_lora), kr_s.reshape(depth, bs, ts, QK_ROPE),
            conv_s.reshape(depth, bs, 2, d_conv))
```

```python
import functools

import jax
import jax.numpy as jnp
from jax import lax
from jax.experimental import pallas as pl
from jax.experimental.pallas import tpu as pltpu

F32, BF16, I32 = jnp.float32, jnp.bfloat16, jnp.int32

N_HEADS = 8
QK_NOPE = 128
QK_ROPE = 64
V_HEAD = 128
ROPE_THETA = 10000.0
EPS = 1e-6
ATTN_SCALE = (QK_NOPE + QK_ROPE) ** -0.5
ROUTE_BLOCK = 256
N_MOD = 6

LANE = 128
SUBLANE = 8
VMEM_LIMIT = 56 * 1024 * 1024

SAMPLE_SPLIT = 4
ROPE_PAD = LANE
DQK = QK_NOPE + ROPE_PAD


def _params(*sem):
    return pltpu.CompilerParams(dimension_semantics=sem, vmem_limit_bytes=VMEM_LIMIT)


def _resident(shape):
    nd = len(shape)
    return pl.BlockSpec(shape, lambda *_: (0,) * nd, pipeline_mode=pl.Buffered(1))


def _dot(a, b):
    return jnp.dot(a, b, preferred_element_type=F32)


def _dot_t(a, b):
    return lax.dot_general(a, b, (((1,), (1,)), ((), ())), preferred_element_type=F32)


def _mm(a, w):
    return _dot(a.astype(BF16), w.astype(BF16))


def _rms(x, g):
    return x * lax.rsqrt(jnp.mean(x * x, axis=-1, keepdims=True) + EPS) * g


def _rope(x, cos, sin):
    n = x.shape[1] // LANE
    if n > 1:
        cos = jnp.concatenate([cos] * n, axis=1)
        sin = jnp.concatenate([sin] * n, axis=1)
    lane = lax.broadcasted_iota(I32, x.shape, 1)
    first_half = (lane % LANE) < (QK_ROPE // 2)
    partner = jnp.where(first_half,
                        pltpu.roll(x, x.shape[1] - QK_ROPE // 2, 1),
                        pltpu.roll(x, QK_ROPE // 2, 1))
    return x * cos + partner * sin


def _mod_row(ref, per_row):
    return ref[...] if per_row else ref[0:1, :]


def _ada_kernel(c_ref, w_ref, b_ref, o_ref):
    o_ref[...] = _mm(jax.nn.silu(c_ref[...]), w_ref[...]) + b_ref[...]


def _ada(c_all, w_ada, b_ada):
    m, d = c_all.shape
    n = w_ada.shape[1]
    tn = 512
    return pl.pallas_call(
        _ada_kernel,
        grid=(n // tn,),
        in_specs=[pl.BlockSpec((m, d), lambda j: (0, 0)),
                  pl.BlockSpec((d, tn), lambda j: (0, j)),
                  pl.BlockSpec((1, tn), lambda j: (0, j))],
        out_specs=pl.BlockSpec((m, tn), lambda j: (0, j)),
        out_shape=jax.ShapeDtypeStruct((m, n), F32),
        compiler_params=_params("arbitrary"),
        name="ada_mod",
    )(c_all, w_ada, b_ada)


def _inproj_kernel(*refs, sample, tm, d_conv, q_lora, kv_lora):
    it = iter(refs)
    x_ref, sh_ref, sc_ref, g_ref = next(it), next(it), next(it), next(it)
    win = next(it)
    cw_ref = next(it)
    if sample:
        st0_ref, st1_ref = next(it), next(it)
    qg_ref, kvg_ref = next(it), next(it)
    wuq = next(it)
    wukv = None if sample else next(it)
    cos_ref, sin_ref = next(it), next(it)
    if sample:
        yconv_ref, qn_ref, qr_ref, ckv_ref, kr_ref, cn0_ref, cn1_ref = it
    else:
        yconv_ref, q_ref, k_ref, v_ref, ckv_ref, kr_ref, cst_ref, carry_ref = it

    def proj(a, lo, hi_col):
        return _mm(a, win[:, lo:hi_col])

    x = x_ref[...]
    h = _rms(x, g_ref[...]) * (1.0 + _mod_row(sc_ref, sample)) + _mod_row(sh_ref, sample)

    c1, c2, c3 = d_conv, 2 * d_conv, 3 * d_conv
    u = proj(h, c2, c3) * proj(h, 0, c1)
    if sample:
        u1, u2 = st1_ref[...], st0_ref[...]
        cn0_ref[...] = u1
        cn1_ref[...] = u
    else:
        @pl.when(pl.program_id(0) == 0)
        def _():
            carry_ref[...] = jnp.zeros_like(carry_ref)
        prev = carry_ref[...]
        p1, p2 = prev[SUBLANE - 1:SUBLANE, :], prev[SUBLANE - 2:SUBLANE - 1, :]
        rows = lax.broadcasted_iota(I32, u.shape, 0)
        u1 = jnp.where(rows == 0, p1, pltpu.roll(u, 1, 0))
        u2 = jnp.where(rows == 0, p2, jnp.where(rows == 1, p1, pltpu.roll(u, 2, 0)))
        tail = u[tm - SUBLANE:tm, :]
        carry_ref[...] = tail
        cst_ref[...] = tail
    conv = cw_ref[0:1, :] * u2 + cw_ref[1:2, :] * u1 + cw_ref[2:3, :] * u
    yconv_ref[...] = (proj(h, c1, c2) * conv).astype(yconv_ref.dtype)

    cos, sin = cos_ref[...], sin_ref[...]
    c4, c5 = c3 + q_lora, c3 + q_lora + kv_lora
    cqn = _rms(proj(h, c3, c4), qg_ref[...])
    ckv = _rms(proj(h, c4, c5), kvg_ref[...])
    kr = _rope(proj(h, c5, c5 + LANE), cos, sin)
    ckv_ref[...] = ckv
    kr_ref[...] = kr[:, :QK_ROPE]
    hq = N_HEADS * QK_NOPE
    q = _mm(cqn, wuq[...])
    if sample:
        qn_ref[...] = q[:, :hq]
        qr_ref[...] = _rope(q[:, hq:], cos, sin)
    else:
        q = q * ATTN_SCALE
        qn = q[:, :hq]
        qr = _rope(q[:, hq:], cos, sin)
        kv = _mm(ckv, wukv[...])
        krb = kr.astype(BF16)
        for hd in range(N_HEADS):
            lo, hi = hd * LANE, (hd + 1) * LANE
            q_ref[hd, :, 0:QK_NOPE] = qn[:, lo:hi].astype(BF16)
            q_ref[hd, :, QK_NOPE:DQK] = qr[:, lo:hi].astype(BF16)
            k_ref[hd, :, 0:QK_NOPE] = kv[:, lo:hi].astype(BF16)
            k_ref[hd, :, QK_NOPE:DQK] = krb
            v_ref[hd] = kv[:, hq + lo:hq + hi].astype(BF16)


def _inproj(x, mod, mod_row_block, norm_g, win, conv_w, state, q_g, kv_g, wuq, wukv, cos, sin, *,
            sample, tm):
    n, d = x.shape
    d_conv = conv_w.shape[1]
    q_lora, kv_lora = q_g.shape[1], kv_g.shape[1]
    rows = tm if sample else SUBLANE
    hq = N_HEADS * QK_NOPE

    def tile(width):
        return pl.BlockSpec((tm, width), lambda i: (i, 0))

    def modspec(col):
        return pl.BlockSpec((rows, d), lambda i: (mod_row_block, col))

    args = [x, mod, mod, norm_g, win, conv_w]
    specs = [tile(d), modspec(0), modspec(1), _resident(norm_g.shape), _resident(win.shape),
             _resident(conv_w.shape)]
    if sample:
        args += list(state)
        specs += [tile(d_conv), tile(d_conv)]
    args += [q_g, kv_g, wuq]
    specs += [_resident(q_g.shape), _resident(kv_g.shape), _resident(wuq.shape)]
    if not sample:
        args.append(wukv)
        specs.append(_resident(wukv.shape))
    args += [cos, sin]
    specs += [tile(LANE), tile(LANE)]

    if sample:
        out_shape = [jax.ShapeDtypeStruct((n, d_conv), F32), jax.ShapeDtypeStruct((n, hq), F32),
                     jax.ShapeDtypeStruct((n, N_HEADS * ROPE_PAD), F32),
                     jax.ShapeDtypeStruct((n, kv_lora), F32), jax.ShapeDtypeStruct((n, QK_ROPE), F32),
                     jax.ShapeDtypeStruct((n, d_conv), F32), jax.ShapeDtypeStruct((n, d_conv), F32)]
        out_specs = [tile(d_conv), tile(hq), tile(N_HEADS * ROPE_PAD), tile(kv_lora), tile(QK_ROPE),
                     tile(d_conv), tile(d_conv)]
        scratch = []
    else:
        def heads(width):
            return pl.BlockSpec((N_HEADS, tm, width), lambda i: (0, i, 0))
        out_shape = [jax.ShapeDtypeStruct((n, d_conv), BF16),
                     jax.ShapeDtypeStruct((N_HEADS, n, DQK), BF16),
                     jax.ShapeDtypeStruct((N_HEADS, n, DQK), BF16),
                     jax.ShapeDtypeStruct((N_HEADS, n, V_HEAD), BF16),
                     jax.ShapeDtypeStruct((n, kv_lora), F32), jax.ShapeDtypeStruct((n, QK_ROPE), F32),
                     jax.ShapeDtypeStruct((SUBLANE, d_conv), F32)]
        out_specs = [tile(d_conv), heads(DQK), heads(DQK), heads(V_HEAD), tile(kv_lora), tile(QK_ROPE),
                     pl.BlockSpec((SUBLANE, d_conv), lambda i: (0, 0))]
        scratch = [pltpu.VMEM((SUBLANE, d_conv), F32)]

    return pl.pallas_call(
        functools.partial(_inproj_kernel, sample=sample, tm=tm, d_conv=d_conv, q_lora=q_lora,
                          kv_lora=kv_lora),
        grid=(n // tm,),
        in_specs=specs, out_specs=out_specs, out_shape=out_shape, scratch_shapes=scratch,
        compiler_params=_params("arbitrary"),
        name="inproj_sample" if sample else "inproj_prompt",
    )(*args)


def _flash_kernel(q_ref, k_ref, v_ref, o_ref, m_sc, l_sc, acc_sc, *, tq, heads):
    qi = pl.program_id(1)
    m_sc[...] = jnp.full_like(m_sc, -jnp.inf)
    l_sc[...] = jnp.zeros_like(l_sc)
    acc_sc[...] = jnp.zeros_like(acc_sc)
    reps = tq // LANE

    def step(ki, masked):
        off = pl.multiple_of(ki * tq, tq)
        for hd in range(heads):
            s = _dot_t(q_ref[hd], k_ref[hd, pl.ds(off, tq), :])
            if masked:
                row = lax.broadcasted_iota(I32, s.shape, 0)
                col = lax.broadcasted_iota(I32, s.shape, 1)
                s = jnp.where(col <= row, s, -jnp.inf)
            m_prev = m_sc[hd]
            m_new = jnp.maximum(m_prev, jnp.max(s, axis=1, keepdims=True))
            alpha = jnp.exp(m_prev - m_new)
            p = jnp.exp(s - jnp.concatenate([m_new] * reps, axis=1))
            l_sc[hd] = alpha * l_sc[hd] + jnp.sum(p, axis=1, keepdims=True)
            acc_sc[hd] = alpha * acc_sc[hd] + _dot(p.astype(BF16), v_ref[hd, pl.ds(off, tq), :])
            m_sc[hd] = m_new

    def body(ki, carry):
        step(ki, False)
        return carry

    lax.fori_loop(0, qi, body, 0)
    step(qi, True)
    for hd in range(heads):
        o_ref[:, hd * V_HEAD:(hd + 1) * V_HEAD] = (acc_sc[hd] / l_sc[hd]).astype(o_ref.dtype)


def _flash(q, k, v, *, tq, heads):
    h, t, _ = q.shape
    assert V_HEAD == LANE and h % heads == 0 and t % tq == 0
    return pl.pallas_call(
        functools.partial(_flash_kernel, tq=tq, heads=heads),
        grid=(h // heads, t // tq),
        in_specs=[pl.BlockSpec((heads, tq, DQK), lambda hh, i: (hh, i, 0)),
                  pl.BlockSpec((heads, t, DQK), lambda hh, i: (hh, 0, 0), pipeline_mode=pl.Buffered(1)),
                  pl.BlockSpec((heads, t, V_HEAD), lambda hh, i: (hh, 0, 0),
                               pipeline_mode=pl.Buffered(1))],
        out_specs=pl.BlockSpec((tq, heads * V_HEAD), lambda hh, i: (i, hh)),
        out_shape=jax.ShapeDtypeStruct((t, h * V_HEAD), BF16),
        scratch_shapes=[pltpu.VMEM((heads, tq, LANE), F32), pltpu.VMEM((heads, tq, LANE), F32),
                        pltpu.VMEM((heads, tq, V_HEAD), F32)],
        compiler_params=_params("arbitrary", "arbitrary"),
        name="flash_prompt",
    )(q, k, v)


def _qlat_kernel(qn_ref, w_ref, o_ref):
    o_ref[...] = _dot_t(qn_ref[...].astype(BF16), w_ref[:, :QK_NOPE].astype(BF16))


def _qlat(qn, w_ukv):
    n = qn.shape[0]
    kv_lora = w_ukv.shape[0]
    per_head = QK_NOPE + V_HEAD
    return pl.pallas_call(
        _qlat_kernel,
        grid=(N_HEADS,),
        in_specs=[pl.BlockSpec((n, QK_NOPE), lambda h: (0, h)),
                  pl.BlockSpec((kv_lora, per_head), lambda h: (0, h))],
        out_specs=pl.BlockSpec((n, kv_lora), lambda h: (0, h)),
        out_shape=jax.ShapeDtypeStruct((n, N_HEADS * kv_lora), F32),
        compiler_params=_params("arbitrary"),
        name="sample_qlat",
    )(qn, w_ukv)


def _uv_kernel(o_ref_in, w_ref, y_ref):
    y_ref[...] = _mm(o_ref_in[...], w_ref[:, QK_NOPE:])


def _uv(o_lat, w_ukv):
    n = o_lat.shape[0]
    kv_lora = w_ukv.shape[0]
    per_head = QK_NOPE + V_HEAD
    return pl.pallas_call(
        _uv_kernel,
        grid=(N_HEADS,),
        in_specs=[pl.BlockSpec((n, kv_lora), lambda h: (0, h)),
                  pl.BlockSpec((kv_lora, per_head), lambda h: (0, h))],
        out_specs=pl.BlockSpec((n, V_HEAD), lambda h: (0, h)),
        out_shape=jax.ShapeDtypeStruct((n, N_HEADS * V_HEAD), F32),
        compiler_params=_params("arbitrary"),
        name="sample_uv",
    )(o_lat, w_ukv)


def _decode_kernel(pt_ref, ql_ref, qr_ref, cn_ref, kn_ref, ckv_hbm, kr_hbm, o_ref,
                   ckv_buf, kr_buf, sem, m_sc, l_sc, acc_sc, *, pages, page):
    s_idx, j = pl.program_id(0), pl.program_id(1)
    n_seq, nj = pl.num_programs(0), pl.num_programs(1)
    step = s_idx * nj + j

    def chunk_copies(seq, chunk, slot):
        out = []
        for p in range(pages):
            pid = pt_ref[seq, chunk * pages + p]
            rows = pl.ds(p * page, page)
            out.append(pltpu.make_async_copy(ckv_hbm.at[pid], ckv_buf.at[slot, rows], sem.at[slot, 0]))
            out.append(pltpu.make_async_copy(kr_hbm.at[pid], kr_buf.at[slot, :, rows], sem.at[slot, 1]))
        return out

    @pl.when(step == 0)
    def _():
        for c in chunk_copies(0, 0, 0):
            c.start()

    @pl.when(step + 1 < n_seq * nj)
    def _():
        wrap = j + 1 == nj
        for c in chunk_copies(jnp.where(wrap, s_idx + 1, s_idx), jnp.where(wrap, 0, j + 1),
                              (step + 1) % 2):
            c.start()

    slot = step % 2
    for c in chunk_copies(s_idx, j, slot):
        c.wait()

    ql = ql_ref[0].astype(BF16).astype(F32)
    qr = qr_ref[0].astype(BF16).astype(F32)

    @pl.when(j == 0)
    def _():
        m_sc[...] = jnp.full_like(m_sc, -jnp.inf)
        l_sc[...] = jnp.zeros_like(l_sc)
        acc_sc[...] = jnp.zeros_like(acc_sc)

    ckv = ckv_buf[slot]
    s = (_dot_t(ql, ckv) + _dot(qr, kr_buf[slot])) * ATTN_SCALE
    m_prev = m_sc[...]
    m_new = jnp.maximum(m_prev, jnp.max(s, axis=1, keepdims=True))
    alpha = jnp.exp(m_prev - m_new)
    pr = jnp.exp(s - m_new)
    l_new = alpha * l_sc[...] + jnp.sum(pr, axis=1, keepdims=True)
    acc = alpha * acc_sc[...] + _dot(pr, ckv)
    m_sc[...] = m_new
    l_sc[...] = l_new
    acc_sc[...] = acc

    @pl.when(j == pl.num_programs(1) - 1)
    def _():
        cn = cn_ref[0].astype(BF16).astype(F32)
        kn = kn_ref[0].astype(BF16).astype(F32)
        s_new = (jnp.sum(ql * cn, axis=1, keepdims=True)
                 + jnp.sum(qr * kn, axis=1, keepdims=True)) * ATTN_SCALE
        m_fin = jnp.maximum(m_new, s_new)
        a_fin = jnp.exp(m_new - m_fin)
        p_new = jnp.exp(s_new - m_fin)
        l_fin = a_fin * l_new + p_new
        o_ref[0] = (a_fin * acc + p_new * cn) / l_fin


def _decode(page_table, q_lat, q_rope, ckv_new, kr_new, cache_ckv, cache_kr_t, *, pages):
    b, n_pages = page_table.shape
    h, kv_lora = q_lat.shape[1:]
    rope = q_rope.shape[2]
    page = cache_ckv.shape[1]

    assert n_pages % pages == 0
    in_specs = [pl.BlockSpec((1, h, kv_lora), lambda s, j, pt: (s, 0, 0)),
                pl.BlockSpec((1, h, rope), lambda s, j, pt: (s, 0, 0)),
                pl.BlockSpec((1, 1, kv_lora), lambda s, j, pt: (s, 0, 0)),
                pl.BlockSpec((1, 1, rope), lambda s, j, pt: (s, 0, 0)),
                pl.BlockSpec(memory_space=pl.ANY),
                pl.BlockSpec(memory_space=pl.ANY)]
    grid_spec = pltpu.PrefetchScalarGridSpec(
        num_scalar_prefetch=1,
        grid=(b, n_pages // pages),
        in_specs=in_specs,
        out_specs=pl.BlockSpec((1, h, kv_lora), lambda s, j, pt: (s, 0, 0)),
        scratch_shapes=[pltpu.VMEM((2, pages * page, kv_lora), F32),
                        pltpu.VMEM((2, rope, pages * page), F32),
                        pltpu.SemaphoreType.DMA((2, 2)),
                        pltpu.VMEM((h, 1), F32), pltpu.VMEM((h, 1), F32),
                        pltpu.VMEM((h, kv_lora), F32)])
    return pl.pallas_call(
        functools.partial(_decode_kernel, pages=pages, page=page),
        grid_spec=grid_spec,
        out_shape=jax.ShapeDtypeStruct((b, h, kv_lora), F32),
        compiler_params=_params("arbitrary", "arbitrary"),
        name="decode_sample",
    )(page_table, q_lat, q_rope, ckv_new, kr_new, cache_ckv, cache_kr_t)


def _attn_kernel(pt_ref, hp_ref, qi_ref, ki_ref,
                 q_ref, k_ref, v_ref, ql_ref, qr_ref, cn_ref, kn_ref, ckv_hbm, kr_hbm,
                 y_ref, o_ref,
                 fm_sc, fl_sc, facc_sc, ckv_buf, kr_buf, sem, dm_sc, dl_sc, dacc_sc,
                 *, tq, heads, pages, page, n_seq, nj):
    del hp_ref
    g = pl.program_id(0)
    n_dec = n_seq * nj
    qi, ki = qi_ref[g], ki_ref[g]
    reps = tq // LANE

    @pl.when(ki == 0)
    def _():
        fm_sc[...] = jnp.full_like(fm_sc, -jnp.inf)
        fl_sc[...] = jnp.zeros_like(fl_sc)
        facc_sc[...] = jnp.zeros_like(facc_sc)

    def prompt_pieces():
        off = pl.multiple_of(ki * tq, tq)
        st = {}

        def scores(hd):
            def emit():
                delta = (lax.broadcasted_iota(I32, (tq, tq), 1)
                         - lax.broadcasted_iota(I32, (tq, tq), 0))
                keep = delta <= jnp.where(ki == qi, 0, tq)
                st[hd] = jnp.where(keep, _dot_t(q_ref[hd], k_ref[hd, pl.ds(off, tq), :]), -jnp.inf)
            return emit

        def softmax(hd):
            def emit():
                s = st.pop(hd)
                m_prev = fm_sc[hd]
                m_new = jnp.maximum(m_prev, jnp.max(s, axis=1, keepdims=True))
                alpha = jnp.exp(m_prev - m_new)
                p = jnp.exp(s - jnp.concatenate([m_new] * reps, axis=1))
                fl_sc[hd] = alpha * fl_sc[hd] + jnp.sum(p, axis=1, keepdims=True)
                fm_sc[hd] = m_new
                st[hd] = (alpha, p.astype(BF16))
            return emit

        def values(hd):
            def emit():
                alpha, p = st.pop(hd)
                facc_sc[hd] = alpha * facc_sc[hd] + _dot(p, v_ref[hd, pl.ds(off, tq), :])
            return emit

        return [f(hd) for hd in range(heads) for f in (scores, softmax, values)]

    def chunk_copies(seq, chunk, slot):
        out = []
        for p in range(pages):
            pid = pt_ref[seq, chunk * pages + p]
            rows = pl.ds(p * page, page)
            out.append(pltpu.make_async_copy(ckv_hbm.at[pid], ckv_buf.at[slot, rows], sem.at[slot, 0]))
            out.append(pltpu.make_async_copy(kr_hbm.at[pid], kr_buf.at[slot, :, rows], sem.at[slot, 1]))
        return out

    def sample_pieces(j, slot):
        first = j == 0
        sub = pages * page // SAMPLE_SPLIT
        st = {"s": []}

        def rows(k):
            return slice(k * sub, (k + 1) * sub)

        def scores(k):
            def emit():
                if k == 0:
                    st["ql"] = ql_ref[0].astype(BF16).astype(F32)
                    st["qr"] = qr_ref[0].astype(BF16).astype(F32)
                st["s"].append((_dot_t(st["ql"], ckv_buf[slot, rows(k), :])
                                + _dot(st["qr"], kr_buf[slot, :, rows(k)])) * ATTN_SCALE)
            return emit

        def softmax():
            s = jnp.concatenate(st.pop("s"), axis=1)
            m_prev = jnp.where(first, -jnp.inf, dm_sc[...])
            m_new = jnp.maximum(m_prev, jnp.max(s, axis=1, keepdims=True))
            alpha = jnp.exp(m_prev - m_new)
            st["pr"] = jnp.exp(s - m_new)
            dm_sc[...] = m_new
            dl_sc[...] = (alpha * jnp.where(first, 0.0, dl_sc[...])
                          + jnp.sum(st["pr"], axis=1, keepdims=True))
            st["acc"] = alpha * jnp.where(first, 0.0, dacc_sc[...])

        def values(k):
            def emit():
                st["acc"] = st["acc"] + _dot(st["pr"][:, rows(k)], ckv_buf[slot, rows(k), :])
                if k == SAMPLE_SPLIT - 1:
                    dacc_sc[...] = st["acc"]
            return emit

        return ([scores(k) for k in range(SAMPLE_SPLIT)] + [softmax]
                + [values(k) for k in range(SAMPLE_SPLIT)])

    @pl.when(g < n_dec)
    def _():
        s_idx, j = g // nj, g % nj

        @pl.when(g == 0)
        def _():
            for c in chunk_copies(0, 0, 0):
                c.start()

        @pl.when(g + 1 < n_dec)
        def _():
            for c in chunk_copies((g + 1) // nj, (g + 1) % nj, (g + 1) % 2):
                c.start()

        slot = g % 2
        for c in chunk_copies(s_idx, j, slot):
            c.wait()
        a, b = sample_pieces(j, slot), prompt_pieces()
        for k in range(max(len(a), len(b))):
            for pieces in (a, b):
                if k < len(pieces):
                    pieces[k]()

        @pl.when(j == nj - 1)
        def _():
            ql = ql_ref[0].astype(BF16).astype(F32)
            qr = qr_ref[0].astype(BF16).astype(F32)
            cn = cn_ref[0].astype(BF16).astype(F32)
            kn = kn_ref[0].astype(BF16).astype(F32)
            s_new = (jnp.sum(ql * cn, axis=1, keepdims=True)
                     + jnp.sum(qr * kn, axis=1, keepdims=True)) * ATTN_SCALE
            m_old = dm_sc[...]
            m_fin = jnp.maximum(m_old, s_new)
            a_fin = jnp.exp(m_old - m_fin)
            p_new = jnp.exp(s_new - m_fin)
            l_fin = a_fin * dl_sc[...] + p_new
            o_ref[0] = (a_fin * dacc_sc[...] + p_new * cn) / l_fin

    @pl.when(g >= n_dec)
    def _():
        for piece in prompt_pieces():
            piece()

    @pl.when(ki == qi)
    def _():
        for hd in range(heads):
            y_ref[:, hd * V_HEAD:(hd + 1) * V_HEAD] = (facc_sc[hd] / fl_sc[hd]).astype(y_ref.dtype)


def _attn(q, k, v, page_table, q_lat, q_rope, ckv_new, kr_new, cache_ckv, cache_kr_t, *, tq, heads,
          pages):
    h, t, _ = q.shape
    n_seq, n_pages = page_table.shape
    hq, kv_lora = q_lat.shape[1:]
    rope = q_rope.shape[2]
    page = cache_ckv.shape[1]
    assert V_HEAD == LANE and h % heads == 0 and t % tq == 0 and n_pages % pages == 0
    nj = n_pages // pages
    n_dec = n_seq * nj
    nq = t // tq
    trips = [(hp, a, b) for hp in range(h // heads) for a in range(nq) for b in range(a + 1)]
    assert len(trips) >= n_dec, "the sample chunks ride on the prompt's attention trips"
    hp_tab, qi_tab, ki_tab = (jnp.asarray([tr[i] for tr in trips], I32) for i in range(3))

    def seq_block(width_shape):
        return pl.BlockSpec(width_shape, lambda g, *_: (jnp.minimum(g // nj, n_seq - 1), 0, 0))

    grid_spec = pltpu.PrefetchScalarGridSpec(
        num_scalar_prefetch=4,
        grid=(len(trips),),
        in_specs=[pl.BlockSpec((heads, tq, DQK), lambda g, pt, hp, qi, ki: (hp[g], qi[g], 0)),
                  pl.BlockSpec((heads, t, DQK), lambda g, pt, hp, qi, ki: (hp[g], 0, 0),
                               pipeline_mode=pl.Buffered(1)),
                  pl.BlockSpec((heads, t, V_HEAD), lambda g, pt, hp, qi, ki: (hp[g], 0, 0),
                               pipeline_mode=pl.Buffered(1)),
                  seq_block((1, hq, kv_lora)), seq_block((1, hq, rope)),
                  seq_block((1, 1, kv_lora)), seq_block((1, 1, rope)),
                  pl.BlockSpec(memory_space=pl.ANY), pl.BlockSpec(memory_space=pl.ANY)],
        out_specs=[pl.BlockSpec((tq, heads * V_HEAD), lambda g, pt, hp, qi, ki: (qi[g], hp[g])),
                   seq_block((1, hq, kv_lora))],
        scratch_shapes=[pltpu.VMEM((heads, tq, LANE), F32), pltpu.VMEM((heads, tq, LANE), F32),
                        pltpu.VMEM((heads, tq, V_HEAD), F32),
                        pltpu.VMEM((2, pages * page, kv_lora), F32),
                        pltpu.VMEM((2, rope, pages * page), F32),
                        pltpu.SemaphoreType.DMA((2, 2)),
                        pltpu.VMEM((hq, 1), F32), pltpu.VMEM((hq, 1), F32),
                        pltpu.VMEM((hq, kv_lora), F32)])
    return pl.pallas_call(
        functools.partial(_attn_kernel, tq=tq, heads=heads, pages=pages, page=page, n_seq=n_seq, nj=nj),
        grid_spec=grid_spec,
        out_shape=[jax.ShapeDtypeStruct((t, h * V_HEAD), BF16),
                   jax.ShapeDtypeStruct((n_seq, hq, kv_lora), F32)],
        compiler_params=_params("arbitrary"),
        name="attention",
    )(page_table, hp_tab, qi_tab, ki_tab, q, k, v, q_lat, q_rope, ckv_new, kr_new, cache_ckv,
      cache_kr_t)


def _oproj_kernel(*refs, sample, tm, d_conv, n_experts, n_groups):
    it = iter(refs)
    x_ref, yc_ref, ya_ref, g1_ref, sh_ref, sc_ref, ng_ref = (next(it) for _ in range(7))
    wo, router, cnt_in_ref = next(it), next(it), next(it)
    x1_ref, h2_ref, info_ref, cnt_ref, carry_ref = it

    d_mix = wo.shape[0]
    o = _mm(yc_ref[...], wo[0:d_conv, :]) + _mm(ya_ref[...], wo[d_conv:d_mix, :])
    x1 = x_ref[...] + _mod_row(g1_ref, sample) * o
    x1_ref[...] = x1
    h2 = _rms(x1, ng_ref[...]) * (1.0 + _mod_row(sc_ref, sample)) + _mod_row(sh_ref, sample)
    h2_ref[...] = h2

    lg = _mm(h2, router[...])
    lane_i = lax.broadcasted_iota(I32, lg.shape, 1)
    lane = lane_i.astype(F32)
    big = float(LANE)
    neg = -jnp.inf
    gmask = (lane_i >= n_experts) & (lane_i < n_experts + n_groups)
    gl = jnp.where(gmask, lg, neg)
    gmax = jnp.max(gl, axis=1, keepdims=True)
    g_sel = jnp.min(jnp.where(gl == gmax, lane, big), axis=1, keepdims=True) - float(n_experts)
    g_w = 1.0 / jnp.sum(jnp.where(gmask, jnp.exp(lg - gmax), 0.0), axis=1, keepdims=True)
    per_group = n_experts // n_groups
    emask = (lane_i < n_experts) & ((lane_i // per_group).astype(F32) == g_sel)
    el = jnp.where(emask, lg, neg)
    v1 = jnp.max(el, axis=1, keepdims=True)
    i1 = jnp.min(jnp.where(el == v1, lane, big), axis=1, keepdims=True)
    el2 = jnp.where(lane == i1, neg, el)
    v2 = jnp.max(el2, axis=1, keepdims=True)
    i2 = jnp.min(jnp.where(el2 == v2, lane, big), axis=1, keepdims=True)
    e2 = jnp.exp(v2 - v1)
    w1 = (1.0 / (1.0 + e2)) * g_w
    w2 = (e2 / (1.0 + e2)) * g_w

    @pl.when(pl.program_id(0) == 0)
    def _():
        carry_ref[...] = cnt_in_ref[...]
    hit1, hit2 = lane == i1, lane == i2
    chosen = jnp.where(hit1 | hit2, 1.0, 0.0)
    row = lax.broadcasted_iota(I32, (tm, tm), 0)
    col = lax.broadcasted_iota(I32, (tm, tm), 1)
    before = jnp.where(col < row, 1.0, 0.0).astype(BF16)
    running = _dot(before, chosen.astype(BF16)) + carry_ref[0:1, :]
    rank1 = jnp.sum(jnp.where(hit1, running, 0.0), axis=1, keepdims=True)
    rank2 = jnp.sum(jnp.where(hit2, running, 0.0), axis=1, keepdims=True)
    total = carry_ref[...] + jnp.sum(chosen, axis=0, keepdims=True)
    carry_ref[...] = total
    cnt_ref[...] = total

    info = jnp.where(lane_i == 0, i1, 0.0)
    info = jnp.where(lane_i == 1, i2, info)
    info = jnp.where(lane_i == 2, rank1, info)
    info = jnp.where(lane_i == 3, rank2, info)
    info = jnp.where(lane_i == 4, w1, info)
    info = jnp.where(lane_i == 5, w2, info)
    info_ref[...] = info


def _oproj(x, yconv, yattn, mod, mod_row_block, norm_g, wo, router, cnt_in, *, sample, tm,
           n_experts, n_groups):
    n, d = x.shape
    d_conv = yconv.shape[1]
    rows = tm if sample else SUBLANE

    def tile(width):
        return pl.BlockSpec((tm, width), lambda i: (i, 0))

    def modspec(col):
        return pl.BlockSpec((rows, d), lambda i: (mod_row_block, col))

    args = [x, yconv, yattn, mod, mod, mod, norm_g]
    specs = [tile(d), tile(d_conv), tile(yattn.shape[1]), modspec(2), modspec(3), modspec(4),
             _resident(norm_g.shape)]
    for w in (wo, router, cnt_in):
        args.append(w)
        specs.append(_resident(w.shape))
    return pl.pallas_call(
        functools.partial(_oproj_kernel, sample=sample, tm=tm, d_conv=d_conv, n_experts=n_experts,
                          n_groups=n_groups),
        grid=(n // tm,),
        in_specs=specs,
        out_specs=[tile(d), tile(d), tile(LANE), pl.BlockSpec((SUBLANE, LANE), lambda i: (0, 0))],
        out_shape=[jax.ShapeDtypeStruct((n, d), F32), jax.ShapeDtypeStruct((n, d), F32),
                   jax.ShapeDtypeStruct((n, LANE), F32), jax.ShapeDtypeStruct((SUBLANE, LANE), F32)],
        scratch_shapes=[pltpu.VMEM((SUBLANE, LANE), F32)],
        compiler_params=_params("arbitrary"),
        name="oproj_sample" if sample else "oproj_prompt",
    )(*args)


def _row_copy(src, src_row, dst, dst_row, sem):
    return pltpu.make_async_copy(src.at[pl.ds(src_row, 1)], dst.at[pl.ds(dst_row, 1)], sem)


def _scatter_kernel(dest_ref, fill_ref, hp_ref, hs_ref, xs_ref, zero_ref, sem, *, tm, n_prompt_tiles):
    i = pl.program_id(0)

    def scatter_rows(h_ref, n_rows, base):
        def start(r, carry):
            t = base + r
            _row_copy(h_ref, r, xs_ref, dest_ref[2 * t], sem).start()
            _row_copy(h_ref, r, xs_ref, dest_ref[2 * t + 1], sem).start()
            return carry

        lax.fori_loop(0, n_rows, start, 0, unroll=8)
        all_rows = xs_ref.at[pl.ds(0, 2 * n_rows)]
        pltpu.make_async_copy(all_rows, all_rows, sem).wait()

    @pl.when(i < n_prompt_tiles)
    def _():
        scatter_rows(hp_ref, tm, i * tm)

    @pl.when(i == n_prompt_tiles)
    def _():
        scatter_rows(hs_ref, hs_ref.shape[0], n_prompt_tiles * tm)

    @pl.when(i == 0)
    def _():
        zero_ref[...] = jnp.zeros_like(zero_ref)

        def each_range(act):
            def per_range(e, carry):
                first, count = fill_ref[2 * e], fill_ref[2 * e + 1]
                lax.fori_loop(0, count, lambda r, c: act(first + r, c), 0)
                return carry
            lax.fori_loop(0, fill_ref.shape[0] // 2, per_range, 0)

        def zstart(slot, c):
            _row_copy(zero_ref, 0, xs_ref, slot, sem).start()
            return c

        def zwait(slot, c):
            _row_copy(zero_ref, 0, xs_ref, 0, sem).wait()
            return c

        each_range(zstart)
        each_range(zwait)


def _scatter(dest, fill, h2_p, h2_s, n_slots, *, tm):
    n_p, d = h2_p.shape
    n_prompt_tiles = n_p // tm
    grid_spec = pltpu.PrefetchScalarGridSpec(
        num_scalar_prefetch=2,
        grid=(n_prompt_tiles + 1,),
        in_specs=[pl.BlockSpec((tm, d), lambda i, *_: (jnp.minimum(i, n_prompt_tiles - 1), 0)),
                  pl.BlockSpec(h2_s.shape, lambda i, *_: (0, 0))],
        out_specs=pl.BlockSpec(memory_space=pl.ANY),
        scratch_shapes=[pltpu.VMEM((SUBLANE, d), F32), pltpu.SemaphoreType.DMA(())])
    return pl.pallas_call(
        functools.partial(_scatter_kernel, tm=tm, n_prompt_tiles=n_prompt_tiles),
        grid_spec=grid_spec,
        out_shape=jax.ShapeDtypeStruct((n_slots, d), F32),
        compiler_params=_params("arbitrary"),
        name="moe_scatter",
    )(dest, fill, h2_p, h2_s)


def _ffn_kernel(blk_e_ref, nused_ref, xs_ref, wg_ref, wu_ref, wd_ref, o_ref):
    del blk_e_ref

    used = pl.program_id(0) < nused_ref[0]

    @pl.when(used)
    def _():
        x = xs_ref[...]
        hdn = jax.nn.silu(_dot(x, wg_ref[0])) * _dot(x, wu_ref[0])
        o_ref[...] = _dot(hdn, wd_ref[0])

    @pl.when(jnp.logical_not(used))
    def _():
        o_ref[...] = jnp.zeros_like(o_ref)


def _ffn(blk_e, nused, xs, w_gate, w_up, w_down):
    n_slots, d = xs.shape
    n_blocks = n_slots // ROUTE_BLOCK
    d_e = w_gate.shape[2]

    def rows(b, be, nu):
        return (jnp.minimum(b, nu[0] - 1), 0)

    grid_spec = pltpu.PrefetchScalarGridSpec(
        num_scalar_prefetch=2,
        grid=(n_blocks,),
        in_specs=[pl.BlockSpec((ROUTE_BLOCK, d), rows),
                  pl.BlockSpec((1, d, d_e), lambda b, be, nu: (be[b], 0, 0)),
                  pl.BlockSpec((1, d, d_e), lambda b, be, nu: (be[b], 0, 0)),
                  pl.BlockSpec((1, d_e, d), lambda b, be, nu: (be[b], 0, 0))],
        out_specs=pl.BlockSpec((ROUTE_BLOCK, d), lambda b, be, nu: (b, 0)))
    return pl.pallas_call(
        _ffn_kernel,
        grid_spec=grid_spec,
        out_shape=jax.ShapeDtypeStruct((n_slots, d), F32),
        compiler_params=_params("arbitrary"),
        name="moe_experts",
    )(blk_e, nused, xs, w_gate, w_up, w_down)


def _final_kernel(dest_ref, x1_ref, info_ref, g2_ref, fg_ref, yb_ref, o_ref, rows_ref, sem, *,
                  tm, base, per_row):
    i = pl.program_id(0)
    n = pl.num_programs(0)

    def issue(tile_idx, slot):
        def start(r, carry):
            t = base + tile_idx * tm + r
            _row_copy(yb_ref, dest_ref[2 * t], rows_ref.at[slot], r, sem.at[slot]).start()
            _row_copy(yb_ref, dest_ref[2 * t + 1], rows_ref.at[slot], tm + r, sem.at[slot]).start()
            return carry
        lax.fori_loop(0, tm, start, 0, unroll=8)

    @pl.when(i == 0)
    def _():
        issue(0, 0)

    @pl.when(i + 1 < n)
    def _():
        issue(i + 1, (i + 1) % 2)

    slot = i % 2

    pltpu.make_async_copy(yb_ref.at[pl.ds(0, 2 * tm)], rows_ref.at[slot], sem.at[slot]).wait()
    info = info_ref[...]
    moe = info[:, 4:5] * rows_ref[slot, 0:tm] + info[:, 5:6] * rows_ref[slot, tm:2 * tm]
    y = x1_ref[...] + _mod_row(g2_ref, per_row) * moe
    o_ref[...] = _rms(y, fg_ref[...])


def _final(dest, x1, info, mod, mod_row_block, final_g, yb, *, tm, base, per_row):
    n, d = x1.shape
    rows = tm if per_row else SUBLANE
    grid_spec = pltpu.PrefetchScalarGridSpec(
        num_scalar_prefetch=1,
        grid=(n // tm,),
        in_specs=[pl.BlockSpec((tm, d), lambda i, *_: (i, 0)),
                  pl.BlockSpec((tm, LANE), lambda i, *_: (i, 0)),
                  pl.BlockSpec((rows, d), lambda i, *_: (mod_row_block, N_MOD - 1)),
                  pl.BlockSpec((1, d), lambda i, *_: (0, 0)),
                  pl.BlockSpec(memory_space=pl.ANY)],
        out_specs=pl.BlockSpec((tm, d), lambda i, *_: (i, 0)),
        scratch_shapes=[pltpu.VMEM((2, 2 * tm, d), F32), pltpu.SemaphoreType.DMA((2,))])
    return pl.pallas_call(
        functools.partial(_final_kernel, tm=tm, base=base, per_row=per_row),
        grid_spec=grid_spec,
        out_shape=jax.ShapeDtypeStruct((n, d), F32),
        compiler_params=_params("arbitrary"),
        name="moe_combine_final",
    )(dest, x1, info, mod, final_g, yb)


def _rope_tables(pos):
    half = QK_ROPE // 2
    inv_freq = ROPE_THETA ** (-jnp.arange(half, dtype=F32) / half)
    ang = pos.astype(F32)[:, None] * inv_freq
    c, s = jnp.cos(ang), jnp.sin(ang)
    z = jnp.zeros((pos.shape[0], LANE - QK_ROPE), F32)
    return jnp.concatenate([c, c, z], axis=1), jnp.concatenate([-s, s, z], axis=1)


def kernel(x_prompt, x_sample, cache_ckv, cache_krope, state_conv, page_table, c_prompt, c_sample,
           w_ada, b_ada, norm_mix_g, norm_ffn_g, w_in, conv_w, q_norm_g, w_uq, kv_norm_g, w_ukv, w_o,
           router_group, router_expert, w_gate, w_up, w_down, final_g):
    depth = w_ada.shape[0]
    bp, tp, d = x_prompt.shape
    bs, ts, _ = x_sample.shape
    assert depth == 1 and bp == 1 and ts == 1, "one layer, one prompt sequence, one new token per sample"
    n_pages = page_table.shape[1]
    page = cache_ckv.shape[2]
    d_conv = conv_w.shape[2]
    q_lora, kv_lora = q_norm_g.shape[1], kv_norm_g.shape[1]
    n_groups = router_group.shape[2]
    n_experts = router_expert.shape[2]
    assert n_experts + n_groups <= LANE and bs % SUBLANE == 0
    tm_p = 256
    assert tp % tm_p == 0 and (2 * tp) % ROUTE_BLOCK == 0

    c_all = jnp.concatenate([c_sample, c_prompt, jnp.zeros((SUBLANE - bp, d), F32)], axis=0)
    mod = _ada(c_all, w_ada[0], b_ada[0][None, :])
    prompt_mod_block = bs // SUBLANE

    win = jnp.pad(w_in[0].astype(BF16), ((0, 0), (0, LANE - QK_ROPE)))
    wuq3 = w_uq[0].astype(BF16).reshape(q_lora, N_HEADS, QK_NOPE + QK_ROPE)
    wuq = jnp.concatenate(
        [wuq3[:, :, :QK_NOPE].reshape(q_lora, N_HEADS * QK_NOPE),
         jnp.pad(wuq3[:, :, QK_NOPE:], ((0, 0), (0, 0), (0, ROPE_PAD - QK_ROPE))).reshape(
             q_lora, N_HEADS * ROPE_PAD)], axis=1)
    wukv3 = w_ukv[0].astype(BF16).reshape(kv_lora, N_HEADS, QK_NOPE + V_HEAD)
    wukv = jnp.concatenate(
        [wukv3[:, :, :QK_NOPE].reshape(kv_lora, N_HEADS * QK_NOPE),
         wukv3[:, :, QK_NOPE:].reshape(kv_lora, N_HEADS * V_HEAD)], axis=1)
    wo = w_o[0].astype(BF16)
    router = jnp.pad(jnp.concatenate([router_expert[0], router_group[0]], axis=1).astype(BF16),
                     ((0, 0), (0, LANE - n_experts - n_groups)))
    g_mix, g_ffn = norm_mix_g[0][None, :], norm_ffn_g[0][None, :]
    q_g, kv_g = q_norm_g[0][None, :], kv_norm_g[0][None, :]

    cos_p, sin_p = _rope_tables(jnp.arange(tp, dtype=I32))
    xp = x_prompt.reshape(tp, d)
    yconv_p, q_p, k_p, v_p, ckv_p, kr_p, cst_p = _inproj(
        xp, mod, prompt_mod_block, g_mix, win, conv_w[0], None, q_g, kv_g, wuq, wukv, cos_p, sin_p,
        sample=False, tm=tm_p)

    past = n_pages * page
    cos_s, sin_s = _rope_tables(jnp.full((bs,), past, I32))
    xs_tok = x_sample.reshape(bs, d)
    yconv_s, qn_s, qr_s, ckv_s, kr_s, cn0_s, cn1_s = _inproj(
        xs_tok, mod, 0, g_mix, win, conv_w[0], (state_conv[0, :, 0], state_conv[0, :, 1]), q_g,
        kv_g, wuq, None, cos_s, sin_s, sample=True, tm=bs)
    q_lat = _qlat(qn_s, w_ukv[0]).reshape(bs, N_HEADS, kv_lora)
    q_rope = qr_s.reshape(bs, N_HEADS, ROPE_PAD)[:, :, :QK_ROPE]
    cache_kr_t = jnp.swapaxes(cache_krope[0], 1, 2)
    yattn_p, o_lat = _attn(q_p, k_p, v_p, page_table, q_lat, q_rope, ckv_s[:, None, :],
                           kr_s[:, None, :], cache_ckv[0], cache_kr_t, tq=512, heads=2, pages=32)
    yattn_s = _uv(o_lat.reshape(bs, N_HEADS * kv_lora), w_ukv[0])

    zeros_cnt = jnp.zeros((SUBLANE, LANE), F32)
    x1_p, h2_p, info_p, cnt_p = _oproj(
        xp, yconv_p, yattn_p, mod, prompt_mod_block, g_ffn, wo, router, zeros_cnt,
        sample=False, tm=tm_p, n_experts=n_experts, n_groups=n_groups)
    x1_s, h2_s, info_s, cnt_s = _oproj(
        xs_tok, yconv_s, yattn_s, mod, 0, g_ffn, wo, router, cnt_p,
        sample=True, tm=bs, n_experts=n_experts, n_groups=n_groups)

    n_tok = tp + bs
    info = jnp.concatenate([info_p[:, :4], info_s[:, :4]], axis=0).astype(I32)
    counts = cnt_s[0, :n_experts].astype(I32)
    padded = (counts + ROUTE_BLOCK - 1) // ROUTE_BLOCK * ROUTE_BLOCK
    pend = jnp.cumsum(padded)
    pstart = pend - padded
    experts = jnp.arange(n_experts, dtype=I32)
    first_slot = jnp.sum(jnp.where(info[:, 0:2, None] == experts, pstart, 0), axis=-1)
    dest = (first_slot + info[:, 2:4]).reshape(-1)
    n_blocks = -(-(n_tok * 2) // ROUTE_BLOCK) + n_experts
    n_slots = n_blocks * ROUTE_BLOCK
    nused = (pend[-1] // ROUTE_BLOCK).astype(I32)
    blk = jnp.minimum(jnp.arange(n_blocks, dtype=I32), nused - 1) * ROUTE_BLOCK
    blk_e = jnp.minimum(jnp.sum((pend[None, :] <= blk[:, None]).astype(I32), axis=1), n_experts - 1)
    fill = jnp.stack([jnp.append(pstart + counts, pend[-1]),
                      jnp.append(padded - counts, n_slots - pend[-1])], axis=1).reshape(-1).astype(I32)

    xs = _scatter(dest, fill, h2_p, h2_s, n_slots, tm=tm_p)
    yb = _ffn(blk_e, nused.reshape(1), xs, w_gate[0], w_up[0], w_down[0])

    fg = final_g[None, :]
    y_p = _final(dest, x1_p, info_p, mod, prompt_mod_block, fg, yb, tm=tm_p, base=0, per_row=False)
    y_s = _final(dest, x1_s, info_s, mod, 0, fg, yb, tm=bs, base=tp, per_row=True)

    conv_p = cst_p[SUBLANE - 2:, :]
    conv_s = jnp.stack([cn0_s, cn1_s], axis=1)
    return (y_p.reshape(bp, tp, d), y_s.reshape(bs, ts, d),
            ckv_p.reshape(depth, bp, tp, kv_lora), kr_p.reshape(depth, bp, tp, QK_ROPE),
            conv_p.reshape(depth, bp, 2, d_conv),
            ckv_s.reshape(depth, bs, ts, kv_lora), kr_s.reshape(depth, bs, ts, QK_ROPE),
            conv_s.reshape(depth, bs, 2, d_conv))
```

```python
import functools

import jax
import jax.numpy as jnp
from jax import lax
from jax.experimental import pallas as pl
from jax.experimental.pallas import tpu as pltpu

F32, BF16, I32 = jnp.float32, jnp.bfloat16, jnp.int32

N_HEADS = 8
QK_NOPE = 128
QK_ROPE = 64
V_HEAD = 128
ROPE_THETA = 10000.0
EPS = 1e-6
ATTN_SCALE = (QK_NOPE + QK_ROPE) ** -0.5
ROUTE_BLOCK = 256
N_MOD = 6

LANE = 128
SUBLANE = 8
VMEM_LIMIT = 56 * 1024 * 1024

SAMPLE_SPLIT = 4
ROPE_PAD = LANE
DQK = QK_NOPE + ROPE_PAD


def _params(*sem):
    return pltpu.CompilerParams(dimension_semantics=sem, vmem_limit_bytes=VMEM_LIMIT)


def _resident(shape):
    nd = len(shape)
    return pl.BlockSpec(shape, lambda *_: (0,) * nd, pipeline_mode=pl.Buffered(1))


def _dot(a, b):
    return jnp.dot(a, b, preferred_element_type=F32)


def _dot_t(a, b):
    return lax.dot_general(a, b, (((1,), (1,)), ((), ())), preferred_element_type=F32)


def _mm(a, w):
    return _dot(a.astype(BF16), w.astype(BF16))


def _rms(x, g):
    return x * lax.rsqrt(jnp.mean(x * x, axis=-1, keepdims=True) + EPS) * g


def _rope(x, cos, sin):
    n = x.shape[1] // LANE
    if n > 1:
        cos = jnp.concatenate([cos] * n, axis=1)
        sin = jnp.concatenate([sin] * n, axis=1)
    lane = lax.broadcasted_iota(I32, x.shape, 1)
    first_half = (lane % LANE) < (QK_ROPE // 2)
    partner = jnp.where(first_half,
                        pltpu.roll(x, x.shape[1] - QK_ROPE // 2, 1),
                        pltpu.roll(x, QK_ROPE // 2, 1))
    return x * cos + partner * sin


def _mod_row(ref, per_row):
    return ref[...] if per_row else ref[0:1, :]


def _ada_kernel(c_ref, w_ref, b_ref, o_ref):
    o_ref[...] = _mm(jax.nn.silu(c_ref[...]), w_ref[...]) + b_ref[...]


def _ada(c_all, w_ada, b_ada):
    m, d = c_all.shape
    n = w_ada.shape[1]
    tn = 512
    return pl.pallas_call(
        _ada_kernel,
        grid=(n // tn,),
        in_specs=[pl.BlockSpec((m, d), lambda j: (0, 0)),
                  pl.BlockSpec((d, tn), lambda j: (0, j)),
                  pl.BlockSpec((1, tn), lambda j: (0, j))],
        out_specs=pl.BlockSpec((m, tn), lambda j: (0, j)),
        out_shape=jax.ShapeDtypeStruct((m, n), F32),
        compiler_params=_params("arbitrary"),
        name="ada_mod",
    )(c_all, w_ada, b_ada)


def _inproj_kernel(*refs, sample, tm, d_conv, q_lora, kv_lora):
    it = iter(refs)
    x_ref, sh_ref, sc_ref, g_ref = next(it), next(it), next(it), next(it)
    win = next(it)
    cw_ref = next(it)
    if sample:
        st0_ref, st1_ref = next(it), next(it)
    qg_ref, kvg_ref = next(it), next(it)
    wuq = next(it)
    wukv = None if sample else next(it)
    cos_ref, sin_ref = next(it), next(it)
    if sample:
        yconv_ref, qn_ref, qr_ref, ckv_ref, kr_ref, cn0_ref, cn1_ref = it
    else:
        yconv_ref, q_ref, k_ref, v_ref, ckv_ref, kr_ref, cst_ref, carry_ref = it

    def proj(a, lo, hi_col):
        return _mm(a, win[:, lo:hi_col])

    x = x_ref[...]
    h = _rms(x, g_ref[...]) * (1.0 + _mod_row(sc_ref, sample)) + _mod_row(sh_ref, sample)

    c1, c2, c3 = d_conv, 2 * d_conv, 3 * d_conv
    u = proj(h, c2, c3) * proj(h, 0, c1)
    if sample:
        u1, u2 = st1_ref[...], st0_ref[...]
        cn0_ref[...] = u1
        cn1_ref[...] = u
    else:
        @pl.when(pl.program_id(0) == 0)
        def _():
            carry_ref[...] = jnp.zeros_like(carry_ref)
        prev = carry_ref[...]
        p1, p2 = prev[SUBLANE - 1:SUBLANE, :], prev[SUBLANE - 2:SUBLANE - 1, :]
        rows = lax.broadcasted_iota(I32, u.shape, 0)
        u1 = jnp.where(rows == 0, p1, pltpu.roll(u, 1, 0))
        u2 = jnp.where(rows == 0, p2, jnp.where(rows == 1, p1, pltpu.roll(u, 2, 0)))
        tail = u[tm - SUBLANE:tm, :]
        carry_ref[...] = tail
        cst_ref[...] = tail
    conv = cw_ref[0:1, :] * u2 + cw_ref[1:2, :] * u1 + cw_ref[2:3, :] * u
    yconv_ref[...] = (proj(h, c1, c2) * conv).astype(yconv_ref.dtype)

    cos, sin = cos_ref[...], sin_ref[...]
    c4, c5 = c3 + q_lora, c3 + q_lora + kv_lora
    cqn = _rms(proj(h, c3, c4), qg_ref[...])
    ckv = _rms(proj(h, c4, c5), kvg_ref[...])
    kr_raw = proj(h, c5, c5 + QK_ROPE)
    kr = _rope(jnp.concatenate([kr_raw, jnp.zeros((tm, LANE - QK_ROPE), F32)], axis=1), cos, sin)
    ckv_ref[...] = ckv
    kr_ref[...] = kr[:, :QK_ROPE]
    hq = N_HEADS * QK_NOPE
    q = _mm(cqn, wuq[...])
    if sample:
        qn_ref[...] = q[:, :hq]
        qr_ref[...] = _rope(q[:, hq:], cos, sin)
    else:
        q = q * ATTN_SCALE
        qn = q[:, :hq]
        qr = _rope(q[:, hq:], cos, sin)
        kv = _mm(ckv, wukv[...])
        krb = kr.astype(BF16)
        for hd in range(N_HEADS):
            lo, hi = hd * LANE, (hd + 1) * LANE
            q_ref[hd, :, 0:QK_NOPE] = qn[:, lo:hi].astype(BF16)
            q_ref[hd, :, QK_NOPE:DQK] = qr[:, lo:hi].astype(BF16)
            k_ref[hd, :, 0:QK_NOPE] = kv[:, lo:hi].astype(BF16)
            k_ref[hd, :, QK_NOPE:DQK] = krb
            v_ref[hd] = kv[:, hq + lo:hq + hi].astype(BF16)


def _inproj(x, mod, mod_row_block, norm_g, win, conv_w, state, q_g, kv_g, wuq, wukv, cos, sin, *,
            sample, tm):
    n, d = x.shape
    d_conv = conv_w.shape[1]
    q_lora, kv_lora = q_g.shape[1], kv_g.shape[1]
    rows = tm if sample else SUBLANE
    hq = N_HEADS * QK_NOPE

    def tile(width):
        return pl.BlockSpec((tm, width), lambda i: (i, 0))

    def modspec(col):
        return pl.BlockSpec((rows, d), lambda i: (mod_row_block, col))

    args = [x, mod, mod, norm_g, win, conv_w]
    specs = [tile(d), modspec(0), modspec(1), _resident(norm_g.shape), _resident(win.shape),
             _resident(conv_w.shape)]
    if sample:
        args += list(state)
        specs += [tile(d_conv), tile(d_conv)]
    args += [q_g, kv_g, wuq]
    specs += [_resident(q_g.shape), _resident(kv_g.shape), _resident(wuq.shape)]
    if not sample:
        args.append(wukv)
        specs.append(_resident(wukv.shape))
    args += [cos, sin]
    specs += [tile(LANE), tile(LANE)]

    if sample:
        out_shape = [jax.ShapeDtypeStruct((n, d_conv), F32), jax.ShapeDtypeStruct((n, hq), F32),
                     jax.ShapeDtypeStruct((n, N_HEADS * ROPE_PAD), F32),
                     jax.ShapeDtypeStruct((n, kv_lora), F32), jax.ShapeDtypeStruct((n, QK_ROPE), F32),
                     jax.ShapeDtypeStruct((n, d_conv), F32), jax.ShapeDtypeStruct((n, d_conv), F32)]
        out_specs = [tile(d_conv), tile(hq), tile(N_HEADS * ROPE_PAD), tile(kv_lora), tile(QK_ROPE),
                     tile(d_conv), tile(d_conv)]
        scratch = []
    else:
        def heads(width):
            return pl.BlockSpec((N_HEADS, tm, width), lambda i: (0, i, 0))
        out_shape = [jax.ShapeDtypeStruct((n, d_conv), BF16),
                     jax.ShapeDtypeStruct((N_HEADS, n, DQK), BF16),
                     jax.ShapeDtypeStruct((N_HEADS, n, DQK), BF16),
                     jax.ShapeDtypeStruct((N_HEADS, n, V_HEAD), BF16),
                     jax.ShapeDtypeStruct((n, kv_lora), F32), jax.ShapeDtypeStruct((n, QK_ROPE), F32),
                     jax.ShapeDtypeStruct((SUBLANE, d_conv), F32)]
        out_specs = [tile(d_conv), heads(DQK), heads(DQK), heads(V_HEAD), tile(kv_lora), tile(QK_ROPE),
                     pl.BlockSpec((SUBLANE, d_conv), lambda i: (0, 0))]
        scratch = [pltpu.VMEM((SUBLANE, d_conv), F32)]

    return pl.pallas_call(
        functools.partial(_inproj_kernel, sample=sample, tm=tm, d_conv=d_conv, q_lora=q_lora,
                          kv_lora=kv_lora),
        grid=(n // tm,),
        in_specs=specs, out_specs=out_specs, out_shape=out_shape, scratch_shapes=scratch,
        compiler_params=_params("arbitrary"),
        name="inproj_sample" if sample else "inproj_prompt",
    )(*args)


def _qlat_kernel(qn_ref, w_ref, o_ref):
    o_ref[...] = _dot_t(qn_ref[...].astype(BF16), w_ref[:, :QK_NOPE].astype(BF16))


def _qlat(qn, w_ukv):
    n = qn.shape[0]
    kv_lora = w_ukv.shape[0]
    per_head = QK_NOPE + V_HEAD
    return pl.pallas_call(
        _qlat_kernel,
        grid=(N_HEADS,),
        in_specs=[pl.BlockSpec((n, QK_NOPE), lambda h: (0, h)),
                  pl.BlockSpec((kv_lora, per_head), lambda h: (0, h))],
        out_specs=pl.BlockSpec((n, kv_lora), lambda h: (0, h)),
        out_shape=jax.ShapeDtypeStruct((n, N_HEADS * kv_lora), F32),
        compiler_params=_params("arbitrary"),
        name="sample_qlat",
    )(qn, w_ukv)


def _uv_kernel(o_ref_in, w_ref, y_ref):
    y_ref[...] = _mm(o_ref_in[...], w_ref[:, QK_NOPE:])


def _uv(o_lat, w_ukv):
    n = o_lat.shape[0]
    kv_lora = w_ukv.shape[0]
    per_head = QK_NOPE + V_HEAD
    return pl.pallas_call(
        _uv_kernel,
        grid=(N_HEADS,),
        in_specs=[pl.BlockSpec((n, kv_lora), lambda h: (0, h)),
                  pl.BlockSpec((kv_lora, per_head), lambda h: (0, h))],
        out_specs=pl.BlockSpec((n, V_HEAD), lambda h: (0, h)),
        out_shape=jax.ShapeDtypeStruct((n, N_HEADS * V_HEAD), F32),
        compiler_params=_params("arbitrary"),
        name="sample_uv",
    )(o_lat, w_ukv)


def _attn_kernel(pt_ref, hp_ref, qi_ref, ki_ref,
                 q_ref, k_ref, v_ref, ql_ref, qr_ref, cn_ref, kn_ref, ckv_hbm, kr_hbm,
                 y_ref, o_ref,
                 fm_sc, fl_sc, facc_sc, bias_sc, ckv_buf, kr_buf, sem, dm_sc, dl_sc, dacc_sc,
                 *, tq, heads, pages, page, n_seq, nj):
    del hp_ref
    g = pl.program_id(0)
    n_dec = n_seq * nj
    qi, ki = qi_ref[g], ki_ref[g]
    reps = tq // LANE

    @pl.when(g == 0)
    def _():
        row = lax.broadcasted_iota(I32, (tq, tq), 0)
        col = lax.broadcasted_iota(I32, (tq, tq), 1)
        bias_sc[0] = jnp.zeros((tq, tq), F32)
        bias_sc[1] = jnp.where(col <= row, 0.0, -jnp.inf)

    @pl.when(ki == 0)
    def _():
        fm_sc[...] = jnp.full_like(fm_sc, -jnp.inf)
        fl_sc[...] = jnp.zeros_like(fl_sc)
        facc_sc[...] = jnp.zeros_like(facc_sc)

    def prompt_pieces():
        off = pl.multiple_of(ki * tq, tq)
        st = {}

        def scores(hd):
            def emit():
                st[hd] = (_dot_t(q_ref[hd], k_ref[hd, pl.ds(off, tq), :])
                          + bias_sc[(ki == qi).astype(I32)])
            return emit

        def softmax(hd):
            def emit():
                s = st.pop(hd)
                m_prev = fm_sc[hd]
                m_new = jnp.maximum(m_prev, jnp.max(s, axis=1, keepdims=True))
                alpha = jnp.exp(m_prev - m_new)
                p = jnp.exp(s - jnp.concatenate([m_new] * reps, axis=1))
                fl_sc[hd] = alpha * fl_sc[hd] + jnp.sum(p, axis=1, keepdims=True)
                fm_sc[hd] = m_new
                st[hd] = (alpha, p.astype(BF16))
            return emit

        def values(hd):
            def emit():
                alpha, p = st.pop(hd)
                facc_sc[hd] = alpha * facc_sc[hd] + _dot(p, v_ref[hd, pl.ds(off, tq), :])
            return emit

        return [f(hd) for hd in range(heads) for f in (scores, softmax, values)]

    def chunk_copies(seq, chunk, slot):
        out = []
        for p in range(pages):
            pid = pt_ref[seq, chunk * pages + p]
            rows = pl.ds(p * page, page)
            out.append(pltpu.make_async_copy(ckv_hbm.at[pid], ckv_buf.at[slot, rows], sem.at[slot, 0]))
            out.append(pltpu.make_async_copy(kr_hbm.at[pid], kr_buf.at[slot, :, rows], sem.at[slot, 1]))
        return out

    def sample_pieces(j, slot):
        first = j == 0
        sub = pages * page // SAMPLE_SPLIT
        st = {"s": []}

        def rows(k):
            return slice(k * sub, (k + 1) * sub)

        def scores(k):
            def emit():
                if k == 0:
                    st["ql"] = ql_ref[0].astype(BF16).astype(F32)
                    st["qr"] = qr_ref[0].astype(BF16).astype(F32)
                st["s"].append((_dot_t(st["ql"], ckv_buf[slot, rows(k), :])
                                + _dot(st["qr"], kr_buf[slot, :, rows(k)])) * ATTN_SCALE)
            return emit

        def softmax():
            s = jnp.concatenate(st.pop("s"), axis=1)
            m_prev = jnp.where(first, -jnp.inf, dm_sc[...])
            m_new = jnp.maximum(m_prev, jnp.max(s, axis=1, keepdims=True))
            alpha = jnp.exp(m_prev - m_new)
            st["pr"] = jnp.exp(s - m_new)
            dm_sc[...] = m_new
            dl_sc[...] = (alpha * jnp.where(first, 0.0, dl_sc[...])
                          + jnp.sum(st["pr"], axis=1, keepdims=True))
            st["acc"] = alpha * jnp.where(first, 0.0, dacc_sc[...])

        def values(k):
            def emit():
                st["acc"] = st["acc"] + _dot(st["pr"][:, rows(k)], ckv_buf[slot, rows(k), :])
                if k == SAMPLE_SPLIT - 1:
                    dacc_sc[...] = st["acc"]
            return emit

        return ([scores(k) for k in range(SAMPLE_SPLIT)] + [softmax]
                + [values(k) for k in range(SAMPLE_SPLIT)])

    @pl.when(g < n_dec)
    def _():
        s_idx, j = g // nj, g % nj

        @pl.when(g == 0)
        def _():
            for c in chunk_copies(0, 0, 0):
                c.start()

        @pl.when(g + 1 < n_dec)
        def _():
            for c in chunk_copies((g + 1) // nj, (g + 1) % nj, (g + 1) % 2):
                c.start()

        slot = g % 2
        for c in chunk_copies(s_idx, j, slot):
            c.wait()
        a, b = sample_pieces(j, slot), prompt_pieces()
        for k in range(max(len(a), len(b))):
            for pieces in (a, b):
                if k < len(pieces):
                    pieces[k]()

        @pl.when(j == nj - 1)
        def _():
            ql = ql_ref[0].astype(BF16).astype(F32)
            qr = qr_ref[0].astype(BF16).astype(F32)
            cn = cn_ref[0].astype(BF16).astype(F32)
            kn = kn_ref[0].astype(BF16).astype(F32)
            s_new = (jnp.sum(ql * cn, axis=1, keepdims=True)
                     + jnp.sum(qr * kn, axis=1, keepdims=True)) * ATTN_SCALE
            m_old = dm_sc[...]
            m_fin = jnp.maximum(m_old, s_new)
            a_fin = jnp.exp(m_old - m_fin)
            p_new = jnp.exp(s_new - m_fin)
            l_fin = a_fin * dl_sc[...] + p_new
            o_ref[0] = (a_fin * dacc_sc[...] + p_new * cn) / l_fin

    @pl.when(g >= n_dec)
    def _():
        for piece in prompt_pieces():
            piece()

    @pl.when(ki == qi)
    def _():
        for hd in range(heads):
            y_ref[:, hd * V_HEAD:(hd + 1) * V_HEAD] = (facc_sc[hd] / fl_sc[hd]).astype(y_ref.dtype)


def _attn(q, k, v, page_table, q_lat, q_rope, ckv_new, kr_new, cache_ckv, cache_kr_t, *, tq, heads,
          pages):
    h, t, _ = q.shape
    n_seq, n_pages = page_table.shape
    hq, kv_lora = q_lat.shape[1:]
    rope = q_rope.shape[2]
    page = cache_ckv.shape[1]
    assert V_HEAD == LANE and h % heads == 0 and t % tq == 0 and n_pages % pages == 0
    nj = n_pages // pages
    n_dec = n_seq * nj
    nq = t // tq
    trips = [(hp, a, b) for hp in range(h // heads) for a in range(nq) for b in range(a + 1)]
    assert len(trips) >= n_dec, "the sample chunks ride on the prompt's attention trips"
    hp_tab, qi_tab, ki_tab = (jnp.asarray([tr[i] for tr in trips], I32) for i in range(3))

    def seq_block(width_shape):
        return pl.BlockSpec(width_shape, lambda g, *_: (jnp.minimum(g // nj, n_seq - 1), 0, 0))

    grid_spec = pltpu.PrefetchScalarGridSpec(
        num_scalar_prefetch=4,
        grid=(len(trips),),
        in_specs=[pl.BlockSpec((heads, tq, DQK), lambda g, pt, hp, qi, ki: (hp[g], qi[g], 0)),
                  pl.BlockSpec((heads, t, DQK), lambda g, pt, hp, qi, ki: (hp[g], 0, 0),
                               pipeline_mode=pl.Buffered(1)),
                  pl.BlockSpec((heads, t, V_HEAD), lambda g, pt, hp, qi, ki: (hp[g], 0, 0),
                               pipeline_mode=pl.Buffered(1)),
                  seq_block((1, hq, kv_lora)), seq_block((1, hq, rope)),
                  seq_block((1, 1, kv_lora)), seq_block((1, 1, rope)),
                  pl.BlockSpec(memory_space=pl.ANY), pl.BlockSpec(memory_space=pl.ANY)],
        out_specs=[pl.BlockSpec((tq, heads * V_HEAD), lambda g, pt, hp, qi, ki: (qi[g], hp[g])),
                   seq_block((1, hq, kv_lora))],
        scratch_shapes=[pltpu.VMEM((heads, tq, LANE), F32), pltpu.VMEM((heads, tq, LANE), F32),
                        pltpu.VMEM((heads, tq, V_HEAD), F32),
                        pltpu.VMEM((2, tq, tq), F32),
                        pltpu.VMEM((2, pages * page, kv_lora), F32),
                        pltpu.VMEM((2, rope, pages * page), F32),
                        pltpu.SemaphoreType.DMA((2, 2)),
                        pltpu.VMEM((hq, 1), F32), pltpu.VMEM((hq, 1), F32),
                        pltpu.VMEM((hq, kv_lora), F32)])
    return pl.pallas_call(
        functools.partial(_attn_kernel, tq=tq, heads=heads, pages=pages, page=page, n_seq=n_seq, nj=nj),
        grid_spec=grid_spec,
        out_shape=[jax.ShapeDtypeStruct((t, h * V_HEAD), BF16),
                   jax.ShapeDtypeStruct((n_seq, hq, kv_lora), F32)],
        compiler_params=_params("arbitrary"),
        name="attention",
    )(page_table, hp_tab, qi_tab, ki_tab, q, k, v, q_lat, q_rope, ckv_new, kr_new, cache_ckv,
      cache_kr_t)


def _oproj_kernel(*refs, sample, tm, d_conv, n_experts, n_groups):
    it = iter(refs)
    x_ref, yc_ref, ya_ref, g1_ref, sh_ref, sc_ref, ng_ref = (next(it) for _ in range(7))
    wo, router, cnt_in_ref = next(it), next(it), next(it)
    x1_ref, h2_ref, info_ref, cnt_ref, carry_ref = it

    d_mix = wo.shape[0]
    o = _mm(yc_ref[...], wo[0:d_conv, :]) + _mm(ya_ref[...], wo[d_conv:d_mix, :])
    x1 = x_ref[...] + _mod_row(g1_ref, sample) * o
    x1_ref[...] = x1
    h2 = _rms(x1, ng_ref[...]) * (1.0 + _mod_row(sc_ref, sample)) + _mod_row(sh_ref, sample)
    h2_ref[...] = h2

    lg = _mm(h2, router[...])
    lane_i = lax.broadcasted_iota(I32, lg.shape, 1)
    lane = lane_i.astype(F32)
    big = float(LANE)
    neg = -jnp.inf
    gmask = (lane_i >= n_experts) & (lane_i < n_experts + n_groups)
    gl = jnp.where(gmask, lg, neg)
    gmax = jnp.max(gl, axis=1, keepdims=True)
    g_sel = jnp.min(jnp.where(gl == gmax, lane, big), axis=1, keepdims=True) - float(n_experts)
    g_w = 1.0 / jnp.sum(jnp.where(gmask, jnp.exp(lg - gmax), 0.0), axis=1, keepdims=True)
    per_group = n_experts // n_groups
    emask = (lane_i < n_experts) & ((lane_i // per_group).astype(F32) == g_sel)
    el = jnp.where(emask, lg, neg)
    v1 = jnp.max(el, axis=1, keepdims=True)
    i1 = jnp.min(jnp.where(el == v1, lane, big), axis=1, keepdims=True)
    el2 = jnp.where(lane == i1, neg, el)
    v2 = jnp.max(el2, axis=1, keepdims=True)
    i2 = jnp.min(jnp.where(el2 == v2, lane, big), axis=1, keepdims=True)
    e2 = jnp.exp(v2 - v1)
    w1 = (1.0 / (1.0 + e2)) * g_w
    w2 = (e2 / (1.0 + e2)) * g_w

    @pl.when(pl.program_id(0) == 0)
    def _():
        carry_ref[...] = cnt_in_ref[...]
    hit1, hit2 = lane == i1, lane == i2
    chosen = jnp.where(hit1 | hit2, 1.0, 0.0)
    row = lax.broadcasted_iota(I32, (tm, tm), 0)
    col = lax.broadcasted_iota(I32, (tm, tm), 1)
    before = jnp.where(col < row, 1.0, 0.0).astype(BF16)
    running = _dot(before, chosen.astype(BF16)) + carry_ref[0:1, :]
    rank1 = jnp.sum(jnp.where(hit1, running, 0.0), axis=1, keepdims=True)
    rank2 = jnp.sum(jnp.where(hit2, running, 0.0), axis=1, keepdims=True)
    total = carry_ref[...] + jnp.sum(chosen, axis=0, keepdims=True)
    carry_ref[...] = total
    cnt_ref[...] = total

    info = jnp.where(lane_i == 0, i1, 0.0)
    info = jnp.where(lane_i == 1, i2, info)
    info = jnp.where(lane_i == 2, rank1, info)
    info = jnp.where(lane_i == 3, rank2, info)
    info = jnp.where(lane_i == 4, w1, info)
    info = jnp.where(lane_i == 5, w2, info)
    info_ref[...] = info


def _oproj(x, yconv, yattn, mod, mod_row_block, norm_g, wo, router, cnt_in, *, sample, tm,
           n_experts, n_groups):
    n, d = x.shape
    d_conv = yconv.shape[1]
    rows = tm if sample else SUBLANE

    def tile(width):
        return pl.BlockSpec((tm, width), lambda i: (i, 0))

    def modspec(col):
        return pl.BlockSpec((rows, d), lambda i: (mod_row_block, col))

    args = [x, yconv, yattn, mod, mod, mod, norm_g]
    specs = [tile(d), tile(d_conv), tile(yattn.shape[1]), modspec(2), modspec(3), modspec(4),
             _resident(norm_g.shape)]
    for w in (wo, router, cnt_in):
        args.append(w)
        specs.append(_resident(w.shape))
    return pl.pallas_call(
        functools.partial(_oproj_kernel, sample=sample, tm=tm, d_conv=d_conv, n_experts=n_experts,
                          n_groups=n_groups),
        grid=(n // tm,),
        in_specs=specs,
        out_specs=[tile(d), tile(d), tile(LANE), pl.BlockSpec((SUBLANE, LANE), lambda i: (0, 0))],
        out_shape=[jax.ShapeDtypeStruct((n, d), F32), jax.ShapeDtypeStruct((n, d), F32),
                   jax.ShapeDtypeStruct((n, LANE), F32), jax.ShapeDtypeStruct((SUBLANE, LANE), F32)],
        scratch_shapes=[pltpu.VMEM((SUBLANE, LANE), F32)],
        compiler_params=_params("arbitrary"),
        name="oproj_sample" if sample else "oproj_prompt",
    )(*args)


def _row_copy(src, src_row, dst, dst_row, sem):
    return pltpu.make_async_copy(src.at[pl.ds(src_row, 1)], dst.at[pl.ds(dst_row, 1)], sem)


def _scatter_kernel(dest_ref, fill_ref, hp_ref, hs_ref, xs_ref, zero_ref, sem, *, tm, n_prompt_tiles):
    i = pl.program_id(0)

    def scatter_rows(h_ref, n_rows, base):
        def start(r, carry):
            t = base + r
            _row_copy(h_ref, r, xs_ref, dest_ref[2 * t], sem).start(priority=0)
            _row_copy(h_ref, r, xs_ref, dest_ref[2 * t + 1], sem).start(priority=1)
            return carry

        lax.fori_loop(0, n_rows, start, 0, unroll=8)
        all_rows = xs_ref.at[pl.ds(0, 2 * n_rows)]
        pltpu.make_async_copy(all_rows, all_rows, sem).wait()

    @pl.when(i < n_prompt_tiles)
    def _():
        scatter_rows(hp_ref, tm, i * tm)

    @pl.when(i == n_prompt_tiles)
    def _():
        scatter_rows(hs_ref, hs_ref.shape[0], n_prompt_tiles * tm)

    @pl.when(i == 0)
    def _():
        zero_ref[...] = jnp.zeros_like(zero_ref)

        def each_range(act):
            def per_range(e, carry):
                first, count = fill_ref[2 * e], fill_ref[2 * e + 1]
                lax.fori_loop(0, count, lambda r, c: act(first + r, c), 0)
                return carry
            lax.fori_loop(0, fill_ref.shape[0] // 2, per_range, 0)

        def zstart(slot, c):
            _row_copy(zero_ref, 0, xs_ref, slot, sem).start()
            return c

        def zwait(slot, c):
            _row_copy(zero_ref, 0, xs_ref, 0, sem).wait()
            return c

        each_range(zstart)
        each_range(zwait)


def _scatter(dest, fill, h2_p, h2_s, n_slots, *, tm):
    n_p, d = h2_p.shape
    n_prompt_tiles = n_p // tm
    grid_spec = pltpu.PrefetchScalarGridSpec(
        num_scalar_prefetch=2,
        grid=(n_prompt_tiles + 1,),
        in_specs=[pl.BlockSpec((tm, d), lambda i, *_: (jnp.minimum(i, n_prompt_tiles - 1), 0)),
                  pl.BlockSpec(h2_s.shape, lambda i, *_: (0, 0))],
        out_specs=pl.BlockSpec(memory_space=pl.ANY),
        scratch_shapes=[pltpu.VMEM((SUBLANE, d), F32), pltpu.SemaphoreType.DMA(())])
    return pl.pallas_call(
        functools.partial(_scatter_kernel, tm=tm, n_prompt_tiles=n_prompt_tiles),
        grid_spec=grid_spec,
        out_shape=jax.ShapeDtypeStruct((n_slots, d), F32),
        compiler_params=_params("arbitrary"),
        name="moe_scatter",
    )(dest, fill, h2_p, h2_s)


def _ffn_kernel(blk_e_ref, nused_ref, wfirst_ref, wslot_ref, wnext_ref, xs_ref, wg_hbm, wu_hbm, wd_hbm,
                o_ref, wg_buf, wu_buf, wd_buf, sem):
    b = pl.program_id(0)

    def weight_copies(e, slot):
        return [pltpu.make_async_copy(wg_hbm.at[e], wg_buf.at[slot], sem.at[slot, 0]),
                pltpu.make_async_copy(wu_hbm.at[e], wu_buf.at[slot], sem.at[slot, 1]),
                pltpu.make_async_copy(wd_hbm.at[e], wd_buf.at[slot], sem.at[slot, 2])]

    @pl.when(b == 0)
    def _():
        for c in weight_copies(blk_e_ref[0], 0):
            c.start()

    slot = wslot_ref[b]

    @pl.when(wfirst_ref[b] == 1)
    def _():
        for c in weight_copies(blk_e_ref[b], slot):
            c.wait()

        @pl.when(wnext_ref[b] >= 0)
        def _():
            for c in weight_copies(wnext_ref[b], 1 - slot):
                c.start()

    used = b < nused_ref[0]

    @pl.when(used)
    def _():
        x = xs_ref[...]
        hdn = jax.nn.silu(_dot(x, wg_buf[slot])) * _dot(x, wu_buf[slot])
        o_ref[...] = _dot(hdn, wd_buf[slot])

    @pl.when(jnp.logical_not(used))
    def _():
        o_ref[...] = jnp.zeros_like(o_ref)


def _ffn(blk_e, nused, xs, w_gate, w_up, w_down):
    n_slots, d = xs.shape
    n_blocks = n_slots // ROUTE_BLOCK
    d_e = w_gate.shape[2]

    blocks = jnp.arange(n_blocks, dtype=I32)
    first = (blocks < nused[0]) & ((blocks == 0) | (blk_e != jnp.roll(blk_e, 1)))
    wslot = (jnp.cumsum(first.astype(I32)) - 1) % 2
    first_pos = jnp.where(first, blocks, n_blocks)
    at_or_after = lax.cummin(first_pos[::-1])[::-1]
    next_pos = jnp.concatenate([at_or_after[1:], jnp.full((1,), n_blocks, I32)])
    wnext = jnp.sum(jnp.where(blocks[None, :] == next_pos[:, None], blk_e[None, :] + 1, 0), axis=1) - 1

    def rows(b, be, nu, *_):
        return (jnp.minimum(b, nu[0] - 1), 0)

    grid_spec = pltpu.PrefetchScalarGridSpec(
        num_scalar_prefetch=5,
        grid=(n_blocks,),
        in_specs=[pl.BlockSpec((ROUTE_BLOCK, d), rows),
                  pl.BlockSpec(memory_space=pl.ANY), pl.BlockSpec(memory_space=pl.ANY),
                  pl.BlockSpec(memory_space=pl.ANY)],
        out_specs=pl.BlockSpec((ROUTE_BLOCK, d), lambda b, *_: (b, 0)),
        scratch_shapes=[pltpu.VMEM((2, d, d_e), F32), pltpu.VMEM((2, d, d_e), F32),
                        pltpu.VMEM((2, d_e, d), F32), pltpu.SemaphoreType.DMA((2, 3))])
    return pl.pallas_call(
        _ffn_kernel,
        grid_spec=grid_spec,
        out_shape=jax.ShapeDtypeStruct((n_slots, d), F32),
        compiler_params=_params("arbitrary"),
        name="moe_experts",
    )(blk_e, nused, first.astype(I32), wslot.astype(I32), wnext.astype(I32), xs, w_gate, w_up, w_down)


def _final_kernel(dest_ref, x1_ref, info_ref, g2_ref, fg_ref, yb_ref, o_ref, rows_ref, sem, *,
                  tm, base, per_row):
    i = pl.program_id(0)
    n = pl.num_programs(0)

    def issue(tile_idx, slot):
        def start(r, carry):
            t = base + tile_idx * tm + r
            _row_copy(yb_ref, dest_ref[2 * t], rows_ref.at[slot], r, sem.at[slot]).start(priority=0)
            _row_copy(yb_ref, dest_ref[2 * t + 1], rows_ref.at[slot], tm + r,
                      sem.at[slot]).start(priority=1)
            return carry
        lax.fori_loop(0, tm, start, 0, unroll=8)

    @pl.when(i == 0)
    def _():
        issue(0, 0)

    @pl.when(i + 1 < n)
    def _():
        issue(i + 1, (i + 1) % 2)

    slot = i % 2

    pltpu.make_async_copy(yb_ref.at[pl.ds(0, 2 * tm)], rows_ref.at[slot], sem.at[slot]).wait()
    info = info_ref[...]
    moe = info[:, 4:5] * rows_ref[slot, 0:tm] + info[:, 5:6] * rows_ref[slot, tm:2 * tm]
    y = x1_ref[...] + _mod_row(g2_ref, per_row) * moe
    o_ref[...] = _rms(y, fg_ref[...])


def _final(dest, x1, info, mod, mod_row_block, final_g, yb, *, tm, base, per_row):
    n, d = x1.shape
    rows = tm if per_row else SUBLANE
    grid_spec = pltpu.PrefetchScalarGridSpec(
        num_scalar_prefetch=1,
        grid=(n // tm,),
        in_specs=[pl.BlockSpec((tm, d), lambda i, *_: (i, 0)),
                  pl.BlockSpec((tm, LANE), lambda i, *_: (i, 0)),
                  pl.BlockSpec((rows, d), lambda i, *_: (mod_row_block, N_MOD - 1)),
                  pl.BlockSpec((1, d), lambda i, *_: (0, 0)),
                  pl.BlockSpec(memory_space=pl.ANY)],
        out_specs=pl.BlockSpec((tm, d), lambda i, *_: (i, 0)),
        scratch_shapes=[pltpu.VMEM((2, 2 * tm, d), F32), pltpu.SemaphoreType.DMA((2,))])
    return pl.pallas_call(
        functools.partial(_final_kernel, tm=tm, base=base, per_row=per_row),
        grid_spec=grid_spec,
        out_shape=jax.ShapeDtypeStruct((n, d), F32),
        compiler_params=_params("arbitrary"),
        name="moe_combine_final",
    )(dest, x1, info, mod, final_g, yb)


def _rope_tables(pos):
    half = QK_ROPE // 2
    inv_freq = ROPE_THETA ** (-jnp.arange(half, dtype=F32) / half)
    ang = pos.astype(F32)[:, None] * inv_freq
    c, s = jnp.cos(ang), jnp.sin(ang)
    z = jnp.zeros((pos.shape[0], LANE - QK_ROPE), F32)
    return jnp.concatenate([c, c, z], axis=1), jnp.concatenate([-s, s, z], axis=1)


def kernel(x_prompt, x_sample, cache_ckv, cache_krope, state_conv, page_table, c_prompt, c_sample,
           w_ada, b_ada, norm_mix_g, norm_ffn_g, w_in, conv_w, q_norm_g, w_uq, kv_norm_g, w_ukv, w_o,
           router_group, router_expert, w_gate, w_up, w_down, final_g):
    depth = w_ada.shape[0]
    bp, tp, d = x_prompt.shape
    bs, ts, _ = x_sample.shape
    assert depth == 1 and bp == 1 and ts == 1, "one layer, one prompt sequence, one new token per sample"
    n_pages = page_table.shape[1]
    page = cache_ckv.shape[2]
    d_conv = conv_w.shape[2]
    q_lora, kv_lora = q_norm_g.shape[1], kv_norm_g.shape[1]
    n_groups = router_group.shape[2]
    n_experts = router_expert.shape[2]
    assert n_experts + n_groups <= LANE and bs % SUBLANE == 0
    tm_p = 256
    assert tp % tm_p == 0 and (2 * tp) % ROUTE_BLOCK == 0

    c_all = jnp.concatenate([c_sample, c_prompt, jnp.zeros((SUBLANE - bp, d), F32)], axis=0)
    mod = _ada(c_all, w_ada[0], b_ada[0][None, :])
    prompt_mod_block = bs // SUBLANE

    win = w_in[0].astype(BF16)
    wuq3 = w_uq[0].astype(BF16).reshape(q_lora, N_HEADS, QK_NOPE + QK_ROPE)
    wuq = jnp.concatenate(
        [wuq3[:, :, :QK_NOPE].reshape(q_lora, N_HEADS * QK_NOPE),
         jnp.pad(wuq3[:, :, QK_NOPE:], ((0, 0), (0, 0), (0, ROPE_PAD - QK_ROPE))).reshape(
             q_lora, N_HEADS * ROPE_PAD)], axis=1)
    wukv3 = w_ukv[0].astype(BF16).reshape(kv_lora, N_HEADS, QK_NOPE + V_HEAD)
    wukv = jnp.concatenate(
        [wukv3[:, :, :QK_NOPE].reshape(kv_lora, N_HEADS * QK_NOPE),
         wukv3[:, :, QK_NOPE:].reshape(kv_lora, N_HEADS * V_HEAD)], axis=1)
    wo = w_o[0].astype(BF16)
    router = jnp.pad(jnp.concatenate([router_expert[0], router_group[0]], axis=1).astype(BF16),
                     ((0, 0), (0, LANE - n_experts - n_groups)))
    g_mix, g_ffn = norm_mix_g[0][None, :], norm_ffn_g[0][None, :]
    q_g, kv_g = q_norm_g[0][None, :], kv_norm_g[0][None, :]

    cos_p, sin_p = _rope_tables(jnp.arange(tp, dtype=I32))
    xp = x_prompt.reshape(tp, d)
    yconv_p, q_p, k_p, v_p, ckv_p, kr_p, cst_p = _inproj(
        xp, mod, prompt_mod_block, g_mix, win, conv_w[0], None, q_g, kv_g, wuq, wukv, cos_p, sin_p,
        sample=False, tm=tm_p)

    past = n_pages * page
    cos_s, sin_s = _rope_tables(jnp.full((bs,), past, I32))
    xs_tok = x_sample.reshape(bs, d)
    yconv_s, qn_s, qr_s, ckv_s, kr_s, cn0_s, cn1_s = _inproj(
        xs_tok, mod, 0, g_mix, win, conv_w[0], (state_conv[0, :, 0], state_conv[0, :, 1]), q_g,
        kv_g, wuq, None, cos_s, sin_s, sample=True, tm=bs)
    q_lat = _qlat(qn_s, w_ukv[0]).reshape(bs, N_HEADS, kv_lora)
    q_rope = qr_s.reshape(bs, N_HEADS, ROPE_PAD)[:, :, :QK_ROPE]
    cache_kr_t = jnp.swapaxes(cache_krope[0], 1, 2)
    yattn_p, o_lat = _attn(q_p, k_p, v_p, page_table, q_lat, q_rope, ckv_s[:, None, :],
                           kr_s[:, None, :], cache_ckv[0], cache_kr_t, tq=512, heads=2, pages=32)
    yattn_s = _uv(o_lat.reshape(bs, N_HEADS * kv_lora), w_ukv[0])

    zeros_cnt = jnp.zeros((SUBLANE, LANE), F32)
    x1_p, h2_p, info_p, cnt_p = _oproj(
        xp, yconv_p, yattn_p, mod, prompt_mod_block, g_ffn, wo, router, zeros_cnt,
        sample=False, tm=tm_p, n_experts=n_experts, n_groups=n_groups)
    x1_s, h2_s, info_s, cnt_s = _oproj(
        xs_tok, yconv_s, yattn_s, mod, 0, g_ffn, wo, router, cnt_p,
        sample=True, tm=bs, n_experts=n_experts, n_groups=n_groups)

    n_tok = tp + bs
    info = jnp.concatenate([info_p[:, :4], info_s[:, :4]], axis=0).astype(I32)
    counts = cnt_s[0, :n_experts].astype(I32)
    padded = (counts + ROUTE_BLOCK - 1) // ROUTE_BLOCK * ROUTE_BLOCK
    pend = jnp.cumsum(padded)
    pstart = pend - padded
    experts = jnp.arange(n_experts, dtype=I32)
    first_slot = jnp.sum(jnp.where(info[:, 0:2, None] == experts, pstart, 0), axis=-1)
    dest = (first_slot + info[:, 2:4]).reshape(-1)
    n_blocks = -(-(n_tok * 2) // ROUTE_BLOCK) + n_experts
    n_slots = n_blocks * ROUTE_BLOCK
    nused = (pend[-1] // ROUTE_BLOCK).astype(I32)
    blk = jnp.minimum(jnp.arange(n_blocks, dtype=I32), nused - 1) * ROUTE_BLOCK
    blk_e = jnp.minimum(jnp.sum((pend[None, :] <= blk[:, None]).astype(I32), axis=1), n_experts - 1)
    fill = jnp.stack([jnp.append(pstart + counts, pend[-1]),
                      jnp.append(padded - counts, n_slots - pend[-1])], axis=1).reshape(-1).astype(I32)

    xs = _scatter(dest, fill, h2_p, h2_s, n_slots, tm=tm_p)
    yb = _ffn(blk_e, nused.reshape(1), xs, w_gate[0], w_up[0], w_down[0])

    fg = final_g[None, :]
    y_p = _final(dest, x1_p, info_p, mod, prompt_mod_block, fg, yb, tm=tm_p, base=0, per_row=False)
    y_s = _final(dest, x1_s, info_s, mod, 0, fg, yb, tm=bs, base=tp, per_row=True)

    conv_p = cst_p[SUBLANE - 2:, :]
    conv_s = jnp.stack([cn0_s, cn1_s], axis=1)
    return (y_p.reshape(bp, tp, d), y_s.reshape(bs, ts, d),
            ckv_p.reshape(depth, bp, tp, kv_lora), kr_p.reshape(depth, bp, tp, QK_ROPE),
            conv_p.reshape(depth, bp, 2, d_conv),
            ckv_s.reshape(depth, bs, ts, kv_lora), kr_s.reshape(depth, bs, ts, QK_ROPE),
            conv_s.reshape(depth, bs, 2, d_conv))
```

```python
import functools

import jax
import jax.numpy as jnp
from jax import lax
from jax.experimental import pallas as pl
from jax.experimental.pallas import tpu as pltpu

F32, BF16, I32 = jnp.float32, jnp.bfloat16, jnp.int32

N_HEADS = 8
QK_NOPE = 128
QK_ROPE = 64
V_HEAD = 128
ROPE_THETA = 10000.0
EPS = 1e-6
ATTN_SCALE = (QK_NOPE + QK_ROPE) ** -0.5
ROUTE_BLOCK = 256
N_MOD = 6

LANE = 128
SUBLANE = 8
VMEM_LIMIT = 56 * 1024 * 1024

SAMPLE_SPLIT = 4
ROPE_PAD = LANE
DQK = QK_NOPE + ROPE_PAD


def _params(*sem):
    return pltpu.CompilerParams(dimension_semantics=sem, vmem_limit_bytes=VMEM_LIMIT)


def _resident(shape):
    nd = len(shape)
    return pl.BlockSpec(shape, lambda *_: (0,) * nd, pipeline_mode=pl.Buffered(1))


def _dot(a, b):
    return jnp.dot(a, b, preferred_element_type=F32)


def _dot_t(a, b):
    return lax.dot_general(a, b, (((1,), (1,)), ((), ())), preferred_element_type=F32)


def _mm(a, w):
    return _dot(a.astype(BF16), w.astype(BF16))


def _rms(x, g):
    return x * lax.rsqrt(jnp.mean(x * x, axis=-1, keepdims=True) + EPS) * g


def _rope(x, cos, sin):
    n = x.shape[1] // LANE
    if n > 1:
        cos = jnp.concatenate([cos] * n, axis=1)
        sin = jnp.concatenate([sin] * n, axis=1)
    lane = lax.broadcasted_iota(I32, x.shape, 1)
    first_half = (lane % LANE) < (QK_ROPE // 2)
    partner = jnp.where(first_half,
                        pltpu.roll(x, x.shape[1] - QK_ROPE // 2, 1),
                        pltpu.roll(x, QK_ROPE // 2, 1))
    return x * cos + partner * sin


def _mod_row(ref, per_row):
    return ref[...] if per_row else ref[0:1, :]


def _ada_kernel(c_ref, w_ref, b_ref, o_ref):
    o_ref[...] = _mm(jax.nn.silu(c_ref[...]), w_ref[...]) + b_ref[...]


def _ada(c_all, w_ada, b_ada):
    m, d = c_all.shape
    n = w_ada.shape[1]
    tn = 512
    return pl.pallas_call(
        _ada_kernel,
        grid=(n // tn,),
        in_specs=[pl.BlockSpec((m, d), lambda j: (0, 0)),
                  pl.BlockSpec((d, tn), lambda j: (0, j)),
                  pl.BlockSpec((1, tn), lambda j: (0, j))],
        out_specs=pl.BlockSpec((m, tn), lambda j: (0, j)),
        out_shape=jax.ShapeDtypeStruct((m, n), F32),
        compiler_params=_params("arbitrary"),
        name="ada_mod",
    )(c_all, w_ada, b_ada)


def _inproj_kernel(*refs, sample, tm, d_conv, q_lora, kv_lora):
    it = iter(refs)
    x_ref, sh_ref, sc_ref, g_ref = next(it), next(it), next(it), next(it)
    win = next(it)
    cw_ref = next(it)
    if sample:
        st0_ref, st1_ref = next(it), next(it)
    qg_ref, kvg_ref = next(it), next(it)
    wuq = next(it)
    wukv = None if sample else next(it)
    cos_ref, sin_ref = next(it), next(it)
    if sample:
        yconv_ref, qn_ref, qr_ref, ckv_ref, kr_ref, cn0_ref, cn1_ref = it
    else:
        yconv_ref, q_ref, k_ref, v_ref, ckv_ref, kr_ref, cst_ref, carry_ref = it

    def proj(a, lo, hi_col):
        return _mm(a, win[:, lo:hi_col])

    x = x_ref[...]
    h = _rms(x, g_ref[...]) * (1.0 + _mod_row(sc_ref, sample)) + _mod_row(sh_ref, sample)

    c1, c2, c3 = d_conv, 2 * d_conv, 3 * d_conv
    u = proj(h, c2, c3) * proj(h, 0, c1)
    if sample:
        u1, u2 = st1_ref[...], st0_ref[...]
        cn0_ref[...] = u1
        cn1_ref[...] = u
    else:
        @pl.when(pl.program_id(0) == 0)
        def _():
            carry_ref[...] = jnp.zeros_like(carry_ref)
        prev = carry_ref[...]
        p1, p2 = prev[SUBLANE - 1:SUBLANE, :], prev[SUBLANE - 2:SUBLANE - 1, :]
        rows = lax.broadcasted_iota(I32, u.shape, 0)
        u1 = jnp.where(rows == 0, p1, pltpu.roll(u, 1, 0))
        u2 = jnp.where(rows == 0, p2, jnp.where(rows == 1, p1, pltpu.roll(u, 2, 0)))
        tail = u[tm - SUBLANE:tm, :]
        carry_ref[...] = tail
        cst_ref[...] = tail
    conv = cw_ref[0:1, :] * u2 + cw_ref[1:2, :] * u1 + cw_ref[2:3, :] * u
    yconv_ref[...] = (proj(h, c1, c2) * conv).astype(yconv_ref.dtype)

    cos, sin = cos_ref[...], sin_ref[...]
    c4, c5 = c3 + q_lora, c3 + q_lora + kv_lora
    cqn = _rms(proj(h, c3, c4), qg_ref[...])
    ckv = _rms(proj(h, c4, c5), kvg_ref[...])
    kr_raw = proj(h, c5, c5 + QK_ROPE)
    kr = _rope(jnp.concatenate([kr_raw, jnp.zeros((tm, LANE - QK_ROPE), F32)], axis=1), cos, sin)
    ckv_ref[...] = ckv
    kr_ref[...] = kr[:, :QK_ROPE]
    hq = N_HEADS * QK_NOPE
    q = _mm(cqn, wuq[...])
    if sample:
        qn_ref[...] = q[:, :hq]
        qr_ref[...] = _rope(q[:, hq:], cos, sin)
    else:
        q = q * ATTN_SCALE
        qn = q[:, :hq]
        qr = _rope(q[:, hq:], cos, sin)
        kv = _mm(ckv, wukv[...])
        krb = kr.astype(BF16)
        for hd in range(N_HEADS):
            lo, hi = hd * LANE, (hd + 1) * LANE
            q_ref[hd, :, 0:QK_NOPE] = qn[:, lo:hi].astype(BF16)
            q_ref[hd, :, QK_NOPE:DQK] = qr[:, lo:hi].astype(BF16)
            k_ref[hd, :, 0:QK_NOPE] = kv[:, lo:hi].astype(BF16)
            k_ref[hd, :, QK_NOPE:DQK] = krb
            v_ref[hd] = kv[:, hq + lo:hq + hi].astype(BF16)


def _inproj(x, mod, mod_row_block, norm_g, win, conv_w, state, q_g, kv_g, wuq, wukv, cos, sin, *,
            sample, tm):
    n, d = x.shape
    d_conv = conv_w.shape[1]
    q_lora, kv_lora = q_g.shape[1], kv_g.shape[1]
    rows = tm if sample else SUBLANE
    hq = N_HEADS * QK_NOPE

    def tile(width):
        return pl.BlockSpec((tm, width), lambda i: (i, 0))

    def modspec(col):
        return pl.BlockSpec((rows, d), lambda i: (mod_row_block, col))

    args = [x, mod, mod, norm_g, win, conv_w]
    specs = [tile(d), modspec(0), modspec(1), _resident(norm_g.shape), _resident(win.shape),
             _resident(conv_w.shape)]
    if sample:
        args += list(state)
        specs += [tile(d_conv), tile(d_conv)]
    args += [q_g, kv_g, wuq]
    specs += [_resident(q_g.shape), _resident(kv_g.shape), _resident(wuq.shape)]
    if not sample:
        args.append(wukv)
        specs.append(_resident(wukv.shape))
    args += [cos, sin]
    specs += [tile(LANE), tile(LANE)]

    if sample:
        out_shape = [jax.ShapeDtypeStruct((n, d_conv), F32), jax.ShapeDtypeStruct((n, hq), F32),
                     jax.ShapeDtypeStruct((n, N_HEADS * ROPE_PAD), F32),
                     jax.ShapeDtypeStruct((n, kv_lora), F32), jax.ShapeDtypeStruct((n, QK_ROPE), F32),
                     jax.ShapeDtypeStruct((n, d_conv), F32), jax.ShapeDtypeStruct((n, d_conv), F32)]
        out_specs = [tile(d_conv), tile(hq), tile(N_HEADS * ROPE_PAD), tile(kv_lora), tile(QK_ROPE),
                     tile(d_conv), tile(d_conv)]
        scratch = []
    else:
        def heads(width):
            return pl.BlockSpec((N_HEADS, tm, width), lambda i: (0, i, 0))
        out_shape = [jax.ShapeDtypeStruct((n, d_conv), BF16),
                     jax.ShapeDtypeStruct((N_HEADS, n, DQK), BF16),
                     jax.ShapeDtypeStruct((N_HEADS, n, DQK), BF16),
                     jax.ShapeDtypeStruct((N_HEADS, n, V_HEAD), BF16),
                     jax.ShapeDtypeStruct((n, kv_lora), F32), jax.ShapeDtypeStruct((n, QK_ROPE), F32),
                     jax.ShapeDtypeStruct((SUBLANE, d_conv), F32)]
        out_specs = [tile(d_conv), heads(DQK), heads(DQK), heads(V_HEAD), tile(kv_lora), tile(QK_ROPE),
                     pl.BlockSpec((SUBLANE, d_conv), lambda i: (0, 0))]
        scratch = [pltpu.VMEM((SUBLANE, d_conv), F32)]

    return pl.pallas_call(
        functools.partial(_inproj_kernel, sample=sample, tm=tm, d_conv=d_conv, q_lora=q_lora,
                          kv_lora=kv_lora),
        grid=(n // tm,),
        in_specs=specs, out_specs=out_specs, out_shape=out_shape, scratch_shapes=scratch,
        compiler_params=_params("arbitrary"),
        name="inproj_sample" if sample else "inproj_prompt",
    )(*args)


def _qlat_kernel(qn_ref, w_ref, o_ref):
    o_ref[...] = _dot_t(qn_ref[...].astype(BF16), w_ref[:, :QK_NOPE].astype(BF16))


def _qlat(qn, w_ukv):
    n = qn.shape[0]
    kv_lora = w_ukv.shape[0]
    per_head = QK_NOPE + V_HEAD
    return pl.pallas_call(
        _qlat_kernel,
        grid=(N_HEADS,),
        in_specs=[pl.BlockSpec((n, QK_NOPE), lambda h: (0, h)),
                  pl.BlockSpec((kv_lora, per_head), lambda h: (0, h))],
        out_specs=pl.BlockSpec((n, kv_lora), lambda h: (0, h)),
        out_shape=jax.ShapeDtypeStruct((n, N_HEADS * kv_lora), F32),
        compiler_params=_params("arbitrary"),
        name="sample_qlat",
    )(qn, w_ukv)


def _uv_kernel(o_ref_in, w_ref, y_ref):
    y_ref[...] = _mm(o_ref_in[...], w_ref[:, QK_NOPE:])


def _uv(o_lat, w_ukv):
    n = o_lat.shape[0]
    kv_lora = w_ukv.shape[0]
    per_head = QK_NOPE + V_HEAD
    return pl.pallas_call(
        _uv_kernel,
        grid=(N_HEADS,),
        in_specs=[pl.BlockSpec((n, kv_lora), lambda h: (0, h)),
                  pl.BlockSpec((kv_lora, per_head), lambda h: (0, h))],
        out_specs=pl.BlockSpec((n, V_HEAD), lambda h: (0, h)),
        out_shape=jax.ShapeDtypeStruct((n, N_HEADS * V_HEAD), F32),
        compiler_params=_params("arbitrary"),
        name="sample_uv",
    )(o_lat, w_ukv)


def _attn_kernel(pt_ref, hp_ref, qi_ref, ki_ref,
                 q_ref, k_ref, v_ref, ql_ref, qr_ref, cn_ref, kn_ref, ckv_hbm, kr_hbm,
                 y_ref, o_ref,
                 fm_sc, fl_sc, facc_sc, bias_sc, ckv_buf, kr_buf, sem, dm_sc, dl_sc, dacc_sc,
                 *, tq, heads, pages, page, n_seq, nj):
    del hp_ref
    g = pl.program_id(0)
    n_dec = n_seq * nj
    qi, ki = qi_ref[g], ki_ref[g]
    reps = tq // LANE

    @pl.when(g == 0)
    def _():
        row = lax.broadcasted_iota(I32, (tq, tq), 0)
        col = lax.broadcasted_iota(I32, (tq, tq), 1)
        bias_sc[0] = jnp.zeros((tq, tq), F32)
        bias_sc[1] = jnp.where(col <= row, 0.0, -jnp.inf)

    @pl.when(ki == 0)
    def _():
        fm_sc[...] = jnp.full_like(fm_sc, -jnp.inf)
        fl_sc[...] = jnp.zeros_like(fl_sc)
        facc_sc[...] = jnp.zeros_like(facc_sc)

    def prompt_pieces():
        off = pl.multiple_of(ki * tq, tq)
        st = {}

        def scores(hd):
            def emit():
                st[hd] = (_dot_t(q_ref[hd], k_ref[hd, pl.ds(off, tq), :])
                          + bias_sc[(ki == qi).astype(I32)])
            return emit

        def softmax(hd):
            def emit():
                s = st.pop(hd)
                m_prev = fm_sc[hd]
                m_new = jnp.maximum(m_prev, jnp.max(s, axis=1, keepdims=True))
                alpha = jnp.exp(m_prev - m_new)
                p = jnp.exp(s - jnp.concatenate([m_new] * reps, axis=1))
                fl_sc[hd] = alpha * fl_sc[hd] + jnp.sum(p, axis=1, keepdims=True)
                fm_sc[hd] = m_new
                st[hd] = (alpha, p.astype(BF16))
            return emit

        def values(hd):
            def emit():
                alpha, p = st.pop(hd)
                facc_sc[hd] = alpha * facc_sc[hd] + _dot(p, v_ref[hd, pl.ds(off, tq), :])
            return emit

        return [f(hd) for hd in range(heads) for f in (scores, softmax, values)]

    def chunk_copies(seq, chunk, slot):
        out = []
        for p in range(pages):
            pid = pt_ref[seq, chunk * pages + p]
            rows = pl.ds(p * page, page)
            out.append(pltpu.make_async_copy(ckv_hbm.at[pid], ckv_buf.at[slot, rows], sem.at[slot, 0]))
            out.append(pltpu.make_async_copy(kr_hbm.at[pid], kr_buf.at[slot, :, rows], sem.at[slot, 1]))
        return out

    def sample_pieces(j, slot):
        first = j == 0
        sub = pages * page // SAMPLE_SPLIT
        st = {"s": []}

        def rows(k):
            return slice(k * sub, (k + 1) * sub)

        def scores(k):
            def emit():
                if k == 0:
                    st["ql"] = ql_ref[0].astype(BF16).astype(F32)
                    st["qr"] = qr_ref[0].astype(BF16).astype(F32)
                st["s"].append((_dot_t(st["ql"], ckv_buf[slot, rows(k), :])
                                + _dot(st["qr"], kr_buf[slot, :, rows(k)])) * ATTN_SCALE)
            return emit

        def softmax():
            s = jnp.concatenate(st.pop("s"), axis=1)
            m_prev = jnp.where(first, -jnp.inf, dm_sc[...])
            m_new = jnp.maximum(m_prev, jnp.max(s, axis=1, keepdims=True))
            alpha = jnp.exp(m_prev - m_new)
            st["pr"] = jnp.exp(s - m_new)
            dm_sc[...] = m_new
            dl_sc[...] = (alpha * jnp.where(first, 0.0, dl_sc[...])
                          + jnp.sum(st["pr"], axis=1, keepdims=True))
            st["acc"] = alpha * jnp.where(first, 0.0, dacc_sc[...])

        def values(k):
            def emit():
                st["acc"] = st["acc"] + _dot(st["pr"][:, rows(k)], ckv_buf[slot, rows(k), :])
                if k == SAMPLE_SPLIT - 1:
                    dacc_sc[...] = st["acc"]
            return emit

        return ([scores(k) for k in range(SAMPLE_SPLIT)] + [softmax]
                + [values(k) for k in range(SAMPLE_SPLIT)])

    @pl.when(g < n_dec)
    def _():
        s_idx, j = g // nj, g % nj

        @pl.when(g == 0)
        def _():
            for c in chunk_copies(0, 0, 0):
                c.start()

        @pl.when(g + 1 < n_dec)
        def _():
            for c in chunk_copies((g + 1) // nj, (g + 1) % nj, (g + 1) % 2):
                c.start()

        slot = g % 2
        for c in chunk_copies(s_idx, j, slot):
            c.wait()
        a, b = sample_pieces(j, slot), prompt_pieces()
        assert heads == 2 and SAMPLE_SPLIT == 4
        order = [b[0], b[3], a[0], b[1], a[1], b[4], a[2], b[2], a[3], b[5]] + a[4:]
        for piece in order:
            piece()

        @pl.when(j == nj - 1)
        def _():
            ql = ql_ref[0].astype(BF16).astype(F32)
            qr = qr_ref[0].astype(BF16).astype(F32)
            cn = cn_ref[0].astype(BF16).astype(F32)
            kn = kn_ref[0].astype(BF16).astype(F32)
            s_new = (jnp.sum(ql * cn, axis=1, keepdims=True)
                     + jnp.sum(qr * kn, axis=1, keepdims=True)) * ATTN_SCALE
            m_old = dm_sc[...]
            m_fin = jnp.maximum(m_old, s_new)
            a_fin = jnp.exp(m_old - m_fin)
            p_new = jnp.exp(s_new - m_fin)
            l_fin = a_fin * dl_sc[...] + p_new
            o_ref[0] = (a_fin * dacc_sc[...] + p_new * cn) / l_fin

    @pl.when(g >= n_dec)
    def _():
        for piece in prompt_pieces():
            piece()

    @pl.when(ki == qi)
    def _():
        for hd in range(heads):
            y_ref[:, hd * V_HEAD:(hd + 1) * V_HEAD] = (facc_sc[hd] / fl_sc[hd]).astype(y_ref.dtype)


def _attn(q, k, v, page_table, q_lat, q_rope, ckv_new, kr_new, cache_ckv, cache_kr_t, *, tq, heads,
          pages):
    h, t, _ = q.shape
    n_seq, n_pages = page_table.shape
    hq, kv_lora = q_lat.shape[1:]
    rope = q_rope.shape[2]
    page = cache_ckv.shape[1]
    assert V_HEAD == LANE and h % heads == 0 and t % tq == 0 and n_pages % pages == 0
    nj = n_pages // pages
    n_dec = n_seq * nj
    nq = t // tq
    trips = [(hp, a, b) for hp in range(h // heads) for a in range(nq) for b in range(a + 1)]
    assert len(trips) >= n_dec, "the sample chunks ride on the prompt's attention trips"
    hp_tab, qi_tab, ki_tab = (jnp.asarray([tr[i] for tr in trips], I32) for i in range(3))

    def seq_block(width_shape):
        return pl.BlockSpec(width_shape, lambda g, *_: (jnp.minimum(g // nj, n_seq - 1), 0, 0))

    grid_spec = pltpu.PrefetchScalarGridSpec(
        num_scalar_prefetch=4,
        grid=(len(trips),),
        in_specs=[pl.BlockSpec((heads, tq, DQK), lambda g, pt, hp, qi, ki: (hp[g], qi[g], 0)),
                  pl.BlockSpec((heads, t, DQK), lambda g, pt, hp, qi, ki: (hp[g], 0, 0),
                               pipeline_mode=pl.Buffered(1)),
                  pl.BlockSpec((heads, t, V_HEAD), lambda g, pt, hp, qi, ki: (hp[g], 0, 0),
                               pipeline_mode=pl.Buffered(1)),
                  seq_block((1, hq, kv_lora)), seq_block((1, hq, rope)),
                  seq_block((1, 1, kv_lora)), seq_block((1, 1, rope)),
                  pl.BlockSpec(memory_space=pl.ANY), pl.BlockSpec(memory_space=pl.ANY)],
        out_specs=[pl.BlockSpec((tq, heads * V_HEAD), lambda g, pt, hp, qi, ki: (qi[g], hp[g])),
                   seq_block((1, hq, kv_lora))],
        scratch_shapes=[pltpu.VMEM((heads, tq, LANE), F32), pltpu.VMEM((heads, tq, LANE), F32),
                        pltpu.VMEM((heads, tq, V_HEAD), F32),
                        pltpu.VMEM((2, tq, tq), F32),
                        pltpu.VMEM((2, pages * page, kv_lora), F32),
                        pltpu.VMEM((2, rope, pages * page), F32),
                        pltpu.SemaphoreType.DMA((2, 2)),
                        pltpu.VMEM((hq, 1), F32), pltpu.VMEM((hq, 1), F32),
                        pltpu.VMEM((hq, kv_lora), F32)])
    return pl.pallas_call(
        functools.partial(_attn_kernel, tq=tq, heads=heads, pages=pages, page=page, n_seq=n_seq, nj=nj),
        grid_spec=grid_spec,
        out_shape=[jax.ShapeDtypeStruct((t, h * V_HEAD), BF16),
                   jax.ShapeDtypeStruct((n_seq, hq, kv_lora), F32)],
        compiler_params=_params("arbitrary"),
        name="attention",
    )(page_table, hp_tab, qi_tab, ki_tab, q, k, v, q_lat, q_rope, ckv_new, kr_new, cache_ckv,
      cache_kr_t)


def _oproj_kernel(*refs, sample, tm, d_conv, n_experts, n_groups):
    it = iter(refs)
    x_ref, yc_ref, ya_ref, g1_ref, sh_ref, sc_ref, ng_ref = (next(it) for _ in range(7))
    wo, router, cnt_in_ref = next(it), next(it), next(it)
    x1_ref, h2_ref, info_ref, cnt_ref, carry_ref = it

    d_mix = wo.shape[0]
    o = _mm(yc_ref[...], wo[0:d_conv, :]) + _mm(ya_ref[...], wo[d_conv:d_mix, :])
    x1 = x_ref[...] + _mod_row(g1_ref, sample) * o
    x1_ref[...] = x1
    h2 = _rms(x1, ng_ref[...]) * (1.0 + _mod_row(sc_ref, sample)) + _mod_row(sh_ref, sample)
    h2_ref[...] = h2

    lg = _mm(h2, router[...])
    lane_i = lax.broadcasted_iota(I32, lg.shape, 1)
    lane = lane_i.astype(F32)
    big = float(LANE)
    neg = -jnp.inf
    gmask = (lane_i >= n_experts) & (lane_i < n_experts + n_groups)
    gl = jnp.where(gmask, lg, neg)
    gmax = jnp.max(gl, axis=1, keepdims=True)
    g_sel = jnp.min(jnp.where(gl == gmax, lane, big), axis=1, keepdims=True) - float(n_experts)
    g_w = 1.0 / jnp.sum(jnp.where(gmask, jnp.exp(lg - gmax), 0.0), axis=1, keepdims=True)
    per_group = n_experts // n_groups
    emask = (lane_i < n_experts) & ((lane_i // per_group).astype(F32) == g_sel)
    el = jnp.where(emask, lg, neg)
    v1 = jnp.max(el, axis=1, keepdims=True)
    i1 = jnp.min(jnp.where(el == v1, lane, big), axis=1, keepdims=True)
    el2 = jnp.where(lane == i1, neg, el)
    v2 = jnp.max(el2, axis=1, keepdims=True)
    i2 = jnp.min(jnp.where(el2 == v2, lane, big), axis=1, keepdims=True)
    e2 = jnp.exp(v2 - v1)
    w1 = (1.0 / (1.0 + e2)) * g_w
    w2 = (e2 / (1.0 + e2)) * g_w

    @pl.when(pl.program_id(0) == 0)
    def _():
        carry_ref[...] = cnt_in_ref[...]
    hit1, hit2 = lane == i1, lane == i2
    chosen = jnp.where(hit1 | hit2, 1.0, 0.0)
    row = lax.broadcasted_iota(I32, (tm, tm), 0)
    col = lax.broadcasted_iota(I32, (tm, tm), 1)
    before = jnp.where(col < row, 1.0, 0.0).astype(BF16)
    running = _dot(before, chosen.astype(BF16)) + carry_ref[0:1, :]
    rank1 = jnp.sum(jnp.where(hit1, running, 0.0), axis=1, keepdims=True)
    rank2 = jnp.sum(jnp.where(hit2, running, 0.0), axis=1, keepdims=True)
    total = carry_ref[...] + jnp.sum(chosen, axis=0, keepdims=True)
    carry_ref[...] = total
    cnt_ref[...] = total

    info = jnp.where(lane_i == 0, i1, 0.0)
    info = jnp.where(lane_i == 1, i2, info)
    info = jnp.where(lane_i == 2, rank1, info)
    info = jnp.where(lane_i == 3, rank2, info)
    info = jnp.where(lane_i == 4, w1, info)
    info = jnp.where(lane_i == 5, w2, info)
    info_ref[...] = info


def _oproj(x, yconv, yattn, mod, mod_row_block, norm_g, wo, router, cnt_in, *, sample, tm,
           n_experts, n_groups):
    n, d = x.shape
    d_conv = yconv.shape[1]
    rows = tm if sample else SUBLANE

    def tile(width):
        return pl.BlockSpec((tm, width), lambda i: (i, 0))

    def modspec(col):
        return pl.BlockSpec((rows, d), lambda i: (mod_row_block, col))

    args = [x, yconv, yattn, mod, mod, mod, norm_g]
    specs = [tile(d), tile(d_conv), tile(yattn.shape[1]), modspec(2), modspec(3), modspec(4),
             _resident(norm_g.shape)]
    for w in (wo, router, cnt_in):
        args.append(w)
        specs.append(_resident(w.shape))
    return pl.pallas_call(
        functools.partial(_oproj_kernel, sample=sample, tm=tm, d_conv=d_conv, n_experts=n_experts,
                          n_groups=n_groups),
        grid=(n // tm,),
        in_specs=specs,
        out_specs=[tile(d), tile(d), tile(LANE), pl.BlockSpec((SUBLANE, LANE), lambda i: (0, 0))],
        out_shape=[jax.ShapeDtypeStruct((n, d), F32), jax.ShapeDtypeStruct((n, d), F32),
                   jax.ShapeDtypeStruct((n, LANE), F32), jax.ShapeDtypeStruct((SUBLANE, LANE), F32)],
        scratch_shapes=[pltpu.VMEM((SUBLANE, LANE), F32)],
        compiler_params=_params("arbitrary"),
        name="oproj_sample" if sample else "oproj_prompt",
    )(*args)


def _row_copy(src, src_row, dst, dst_row, sem):
    return pltpu.make_async_copy(src.at[pl.ds(src_row, 1)], dst.at[pl.ds(dst_row, 1)], sem)


def _scatter_kernel(dest_ref, fill_ref, hp_ref, hs_ref, xs_ref, zero_ref, sem, *, tm, n_prompt_tiles):
    i = pl.program_id(0)

    def scatter_rows(h_ref, n_rows, base):
        def start(r, carry):
            t = base + r
            _row_copy(h_ref, r, xs_ref, dest_ref[2 * t], sem).start()
            _row_copy(h_ref, r, xs_ref, dest_ref[2 * t + 1], sem).start()
            return carry

        lax.fori_loop(0, n_rows, start, 0, unroll=8)
        all_rows = xs_ref.at[pl.ds(0, 2 * n_rows)]
        pltpu.make_async_copy(all_rows, all_rows, sem).wait()

    @pl.when(i < n_prompt_tiles)
    def _():
        scatter_rows(hp_ref, tm, i * tm)

    @pl.when(i == n_prompt_tiles)
    def _():
        scatter_rows(hs_ref, hs_ref.shape[0], n_prompt_tiles * tm)

    @pl.when(i == 0)
    def _():
        zero_ref[...] = jnp.zeros_like(zero_ref)

        def each_range(act):
            def per_range(e, carry):
                first, count = fill_ref[2 * e], fill_ref[2 * e + 1]
                lax.fori_loop(0, count, lambda r, c: act(first + r, c), 0)
                return carry
            lax.fori_loop(0, fill_ref.shape[0] // 2, per_range, 0)

        def zstart(slot, c):
            _row_copy(zero_ref, 0, xs_ref, slot, sem).start()
            return c

        def zwait(slot, c):
            _row_copy(zero_ref, 0, xs_ref, 0, sem).wait()
            return c

        each_range(zstart)
        each_range(zwait)


def _scatter(dest, fill, h2_p, h2_s, n_slots, *, tm):
    n_p, d = h2_p.shape
    n_prompt_tiles = n_p // tm
    grid_spec = pltpu.PrefetchScalarGridSpec(
        num_scalar_prefetch=2,
        grid=(n_prompt_tiles + 1,),
        in_specs=[pl.BlockSpec((tm, d), lambda i, *_: (jnp.minimum(i, n_prompt_tiles - 1), 0)),
                  pl.BlockSpec(h2_s.shape, lambda i, *_: (0, 0))],
        out_specs=pl.BlockSpec(memory_space=pl.ANY),
        scratch_shapes=[pltpu.VMEM((SUBLANE, d), F32), pltpu.SemaphoreType.DMA(())])
    return pl.pallas_call(
        functools.partial(_scatter_kernel, tm=tm, n_prompt_tiles=n_prompt_tiles),
        grid_spec=grid_spec,
        out_shape=jax.ShapeDtypeStruct((n_slots, d), F32),
        compiler_params=_params("arbitrary"),
        name="moe_scatter",
    )(dest, fill, h2_p, h2_s)


def _ffn_kernel(blk_e_ref, nused_ref, wfirst_ref, wslot_ref, wnext_ref, xs_ref, wg_hbm, wu_hbm, wd_hbm,
                o_ref, wg_buf, wu_buf, wd_buf, sem):
    b = pl.program_id(0)

    def weight_copies(e, slot):
        return [pltpu.make_async_copy(wg_hbm.at[e], wg_buf.at[slot], sem.at[slot, 0]),
                pltpu.make_async_copy(wu_hbm.at[e], wu_buf.at[slot], sem.at[slot, 1]),
                pltpu.make_async_copy(wd_hbm.at[e], wd_buf.at[slot], sem.at[slot, 2])]

    @pl.when(b == 0)
    def _():
        for c in weight_copies(blk_e_ref[0], 0):
            c.start()

    slot = wslot_ref[b]

    @pl.when(wfirst_ref[b] == 1)
    def _():
        for c in weight_copies(blk_e_ref[b], slot):
            c.wait()

        @pl.when(wnext_ref[b] >= 0)
        def _():
            for c in weight_copies(wnext_ref[b], 1 - slot):
                c.start()

    used = b < nused_ref[0]

    @pl.when(used)
    def _():
        x = xs_ref[...]
        hdn = jax.nn.silu(_dot(x, wg_buf[slot])) * _dot(x, wu_buf[slot])
        o_ref[...] = _dot(hdn, wd_buf[slot])

    @pl.when(jnp.logical_not(used))
    def _():
        o_ref[...] = jnp.zeros_like(o_ref)


def _ffn(blk_e, nused, xs, w_gate, w_up, w_down):
    n_slots, d = xs.shape
    n_blocks = n_slots // ROUTE_BLOCK
    d_e = w_gate.shape[2]

    blocks = jnp.arange(n_blocks, dtype=I32)
    first = (blocks < nused[0]) & ((blocks == 0) | (blk_e != jnp.roll(blk_e, 1)))
    wslot = (jnp.cumsum(first.astype(I32)) - 1) % 2
    first_pos = jnp.where(first, blocks, n_blocks)
    at_or_after = lax.cummin(first_pos[::-1])[::-1]
    next_pos = jnp.concatenate([at_or_after[1:], jnp.full((1,), n_blocks, I32)])
    wnext = jnp.sum(jnp.where(blocks[None, :] == next_pos[:, None], blk_e[None, :] + 1, 0), axis=1) - 1

    def rows(b, be, nu, *_):
        return (jnp.minimum(b, nu[0] - 1), 0)

    grid_spec = pltpu.PrefetchScalarGridSpec(
        num_scalar_prefetch=5,
        grid=(n_blocks,),
        in_specs=[pl.BlockSpec((ROUTE_BLOCK, d), rows),
                  pl.BlockSpec(memory_space=pl.ANY), pl.BlockSpec(memory_space=pl.ANY),
                  pl.BlockSpec(memory_space=pl.ANY)],
        out_specs=pl.BlockSpec((ROUTE_BLOCK, d), lambda b, *_: (b, 0)),
        scratch_shapes=[pltpu.VMEM((2, d, d_e), F32), pltpu.VMEM((2, d, d_e), F32),
                        pltpu.VMEM((2, d_e, d), F32), pltpu.SemaphoreType.DMA((2, 3))])
    return pl.pallas_call(
        _ffn_kernel,
        grid_spec=grid_spec,
        out_shape=jax.ShapeDtypeStruct((n_slots, d), F32),
        compiler_params=_params("arbitrary"),
        name="moe_experts",
    )(blk_e, nused, first.astype(I32), wslot.astype(I32), wnext.astype(I32), xs, w_gate, w_up, w_down)


def _final_kernel(dest_ref, x1_ref, info_ref, g2_ref, fg_ref, yb_ref, o_ref, rows_ref, sem, *,
                  tm, base, per_row):
    i = pl.program_id(0)
    n = pl.num_programs(0)

    def issue(tile_idx, slot):
        def start(r, carry):
            t = base + tile_idx * tm + r
            _row_copy(yb_ref, dest_ref[2 * t], rows_ref.at[slot], r, sem.at[slot]).start()
            _row_copy(yb_ref, dest_ref[2 * t + 1], rows_ref.at[slot], tm + r, sem.at[slot]).start()
            return carry
        lax.fori_loop(0, tm, start, 0, unroll=8)

    @pl.when(i == 0)
    def _():
        issue(0, 0)

    @pl.when(i + 1 < n)
    def _():
        issue(i + 1, (i + 1) % 2)

    slot = i % 2

    pltpu.make_async_copy(yb_ref.at[pl.ds(0, 2 * tm)], rows_ref.at[slot], sem.at[slot]).wait()
    info = info_ref[...]
    moe = info[:, 4:5] * rows_ref[slot, 0:tm] + info[:, 5:6] * rows_ref[slot, tm:2 * tm]
    y = x1_ref[...] + _mod_row(g2_ref, per_row) * moe
    o_ref[...] = _rms(y, fg_ref[...])


def _final(dest, x1, info, mod, mod_row_block, final_g, yb, *, tm, base, per_row):
    n, d = x1.shape
    rows = tm if per_row else SUBLANE
    grid_spec = pltpu.PrefetchScalarGridSpec(
        num_scalar_prefetch=1,
        grid=(n // tm,),
        in_specs=[pl.BlockSpec((tm, d), lambda i, *_: (i, 0)),
                  pl.BlockSpec((tm, LANE), lambda i, *_: (i, 0)),
                  pl.BlockSpec((rows, d), lambda i, *_: (mod_row_block, N_MOD - 1)),
                  pl.BlockSpec((1, d), lambda i, *_: (0, 0)),
                  pl.BlockSpec(memory_space=pl.ANY)],
        out_specs=pl.BlockSpec((tm, d), lambda i, *_: (i, 0)),
        scratch_shapes=[pltpu.VMEM((2, 2 * tm, d), F32), pltpu.SemaphoreType.DMA((2,))])
    return pl.pallas_call(
        functools.partial(_final_kernel, tm=tm, base=base, per_row=per_row),
        grid_spec=grid_spec,
        out_shape=jax.ShapeDtypeStruct((n, d), F32),
        compiler_params=_params("arbitrary"),
        name="moe_combine_final",
    )(dest, x1, info, mod, final_g, yb)


def _rope_tables(pos):
    half = QK_ROPE // 2
    inv_freq = ROPE_THETA ** (-jnp.arange(half, dtype=F32) / half)
    ang = pos.astype(F32)[:, None] * inv_freq
    c, s = jnp.cos(ang), jnp.sin(ang)
    z = jnp.zeros((pos.shape[0], LANE - QK_ROPE), F32)
    return jnp.concatenate([c, c, z], axis=1), jnp.concatenate([-s, s, z], axis=1)


def kernel(x_prompt, x_sample, cache_ckv, cache_krope, state_conv, page_table, c_prompt, c_sample,
           w_ada, b_ada, norm_mix_g, norm_ffn_g, w_in, conv_w, q_norm_g, w_uq, kv_norm_g, w_ukv, w_o,
           router_group, router_expert, w_gate, w_up, w_down, final_g):
    depth = w_ada.shape[0]
    bp, tp, d = x_prompt.shape
    bs, ts, _ = x_sample.shape
    assert depth == 1 and bp == 1 and ts == 1, "one layer, one prompt sequence, one new token per sample"
    n_pages = page_table.shape[1]
    page = cache_ckv.shape[2]
    d_conv = conv_w.shape[2]
    q_lora, kv_lora = q_norm_g.shape[1], kv_norm_g.shape[1]
    n_groups = router_group.shape[2]
    n_experts = router_expert.shape[2]
    assert n_experts + n_groups <= LANE and bs % SUBLANE == 0
    tm_p = 256
    assert tp % tm_p == 0 and (2 * tp) % ROUTE_BLOCK == 0

    c_all = jnp.concatenate([c_sample, c_prompt, jnp.zeros((SUBLANE - bp, d), F32)], axis=0)
    mod = _ada(c_all, w_ada[0], b_ada[0][None, :])
    prompt_mod_block = bs // SUBLANE

    win = w_in[0].astype(BF16)
    wuq3 = w_uq[0].astype(BF16).reshape(q_lora, N_HEADS, QK_NOPE + QK_ROPE)
    wuq = jnp.concatenate(
        [wuq3[:, :, :QK_NOPE].reshape(q_lora, N_HEADS * QK_NOPE),
         jnp.pad(wuq3[:, :, QK_NOPE:], ((0, 0), (0, 0), (0, ROPE_PAD - QK_ROPE))).reshape(
             q_lora, N_HEADS * ROPE_PAD)], axis=1)
    wukv3 = w_ukv[0].astype(BF16).reshape(kv_lora, N_HEADS, QK_NOPE + V_HEAD)
    wukv = jnp.concatenate(
        [wukv3[:, :, :QK_NOPE].reshape(kv_lora, N_HEADS * QK_NOPE),
         wukv3[:, :, QK_NOPE:].reshape(kv_lora, N_HEADS * V_HEAD)], axis=1)
    wo = w_o[0].astype(BF16)
    router = jnp.pad(jnp.concatenate([router_expert[0], router_group[0]], axis=1).astype(BF16),
                     ((0, 0), (0, LANE - n_experts - n_groups)))
    g_mix, g_ffn = norm_mix_g[0][None, :], norm_ffn_g[0][None, :]
    q_g, kv_g = q_norm_g[0][None, :], kv_norm_g[0][None, :]

    cos_p, sin_p = _rope_tables(jnp.arange(tp, dtype=I32))
    xp = x_prompt.reshape(tp, d)
    yconv_p, q_p, k_p, v_p, ckv_p, kr_p, cst_p = _inproj(
        xp, mod, prompt_mod_block, g_mix, win, conv_w[0], None, q_g, kv_g, wuq, wukv, cos_p, sin_p,
        sample=False, tm=tm_p)

    past = n_pages * page
    cos_s, sin_s = _rope_tables(jnp.full((bs,), past, I32))
    xs_tok = x_sample.reshape(bs, d)
    yconv_s, qn_s, qr_s, ckv_s, kr_s, cn0_s, cn1_s = _inproj(
        xs_tok, mod, 0, g_mix, win, conv_w[0], (state_conv[0, :, 0], state_conv[0, :, 1]), q_g,
        kv_g, wuq, None, cos_s, sin_s, sample=True, tm=bs)
    q_lat = _qlat(qn_s, w_ukv[0]).reshape(bs, N_HEADS, kv_lora)
    q_rope = qr_s.reshape(bs, N_HEADS, ROPE_PAD)[:, :, :QK_ROPE]
    cache_kr_t = jnp.swapaxes(cache_krope[0], 1, 2)
    yattn_p, o_lat = _attn(q_p, k_p, v_p, page_table, q_lat, q_rope, ckv_s[:, None, :],
                           kr_s[:, None, :], cache_ckv[0], cache_kr_t, tq=512, heads=2, pages=32)
    yattn_s = _uv(o_lat.reshape(bs, N_HEADS * kv_lora), w_ukv[0])

    zeros_cnt = jnp.zeros((SUBLANE, LANE), F32)
    x1_p, h2_p, info_p, cnt_p = _oproj(
        xp, yconv_p, yattn_p, mod, prompt_mod_block, g_ffn, wo, router, zeros_cnt,
        sample=False, tm=tm_p, n_experts=n_experts, n_groups=n_groups)
    x1_s, h2_s, info_s, cnt_s = _oproj(
        xs_tok, yconv_s, yattn_s, mod, 0, g_ffn, wo, router, cnt_p,
        sample=True, tm=bs, n_experts=n_experts, n_groups=n_groups)

    n_tok = tp + bs
    info = jnp.concatenate([info_p[:, :4], info_s[:, :4]], axis=0).astype(I32)
    counts = cnt_s[0, :n_experts].astype(I32)
    padded = (counts + ROUTE_BLOCK - 1) // ROUTE_BLOCK * ROUTE_BLOCK
    pend = jnp.cumsum(padded)
    pstart = pend - padded
    experts = jnp.arange(n_experts, dtype=I32)
    first_slot = jnp.sum(jnp.where(info[:, 0:2, None] == experts, pstart, 0), axis=-1)
    dest = (first_slot + info[:, 2:4]).reshape(-1)
    n_blocks = -(-(n_tok * 2) // ROUTE_BLOCK) + n_experts
    n_slots = n_blocks * ROUTE_BLOCK
    nused = (pend[-1] // ROUTE_BLOCK).astype(I32)
    blk = jnp.minimum(jnp.arange(n_blocks, dtype=I32), nused - 1) * ROUTE_BLOCK
    blk_e = jnp.minimum(jnp.sum((pend[None, :] <= blk[:, None]).astype(I32), axis=1), n_experts - 1)
    fill = jnp.stack([jnp.append(pstart + counts, pend[-1]),
                      jnp.append(padded - counts, n_slots - pend[-1])], axis=1).reshape(-1).astype(I32)

    xs = _scatter(dest, fill, h2_p, h2_s, n_slots, tm=tm_p)
    yb = _ffn(blk_e, nused.reshape(1), xs, w_gate[0], w_up[0], w_down[0])

    fg = final_g[None, :]
    y_p = _final(dest, x1_p, info_p, mod, prompt_mod_block, fg, yb, tm=tm_p, base=0, per_row=False)
    y_s = _final(dest, x1_s, info_s, mod, 0, fg, yb, tm=bs, base=tp, per_row=True)

    conv_p = cst_p[SUBLANE - 2:, :]
    conv_s = jnp.stack([cn0_s, cn1_s], axis=1)
    return (y_p.reshape(bp, tp, d), y_s.reshape(bs, ts, d),
            ckv_p.reshape(depth, bp, tp, kv_lora), kr_p.reshape(depth, bp, tp, QK_ROPE),
            conv_p.reshape(depth, bp, 2, d_conv),
            ckv_s.reshape(depth, bs, ts, kv_lora), kr_s.reshape(depth, bs, ts, QK_ROPE),
            conv_s.reshape(depth, bs, 2, d_conv))
```

```python
import functools

import jax
import jax.numpy as jnp
from jax import lax
from jax.experimental import pallas as pl
from jax.experimental.pallas import tpu as pltpu

F32, BF16, I32 = jnp.float32, jnp.bfloat16, jnp.int32

N_HEADS = 8
QK_NOPE = 128
QK_ROPE = 64
V_HEAD = 128
ROPE_THETA = 10000.0
EPS = 1e-6
ATTN_SCALE = (QK_NOPE + QK_ROPE) ** -0.5
ROUTE_BLOCK = 256
N_MOD = 6

LANE = 128
SUBLANE = 8
VMEM_LIMIT = 56 * 1024 * 1024

SAMPLE_SPLIT = 4
ROPE_PAD = LANE
DQK = QK_NOPE + ROPE_PAD


def _params(*sem):
    return pltpu.CompilerParams(dimension_semantics=sem, vmem_limit_bytes=VMEM_LIMIT)


def _resident(shape):
    nd = len(shape)
    return pl.BlockSpec(shape, lambda *_: (0,) * nd, pipeline_mode=pl.Buffered(1))


def _dot(a, b):
    return jnp.dot(a, b, preferred_element_type=F32)


def _dot_t(a, b):
    return lax.dot_general(a, b, (((1,), (1,)), ((), ())), preferred_element_type=F32)


def _mm(a, w):
    return _dot(a.astype(BF16), w.astype(BF16))


def _rms(x, g):
    return x * lax.rsqrt(jnp.mean(x * x, axis=-1, keepdims=True) + EPS) * g


def _rope(x, cos, sin):
    n = x.shape[1] // LANE
    if n > 1:
        cos = jnp.concatenate([cos] * n, axis=1)
        sin = jnp.concatenate([sin] * n, axis=1)
    lane = lax.broadcasted_iota(I32, x.shape, 1)
    first_half = (lane % LANE) < (QK_ROPE // 2)
    partner = jnp.where(first_half,
                        pltpu.roll(x, x.shape[1] - QK_ROPE // 2, 1),
                        pltpu.roll(x, QK_ROPE // 2, 1))
    return x * cos + partner * sin


def _mod_row(ref, per_row):
    return ref[...] if per_row else ref[0:1, :]


def _ada_kernel(c_ref, w_ref, b_ref, o_ref):
    o_ref[...] = _mm(jax.nn.silu(c_ref[...]), w_ref[...]) + b_ref[...]


def _ada(c_all, w_ada, b_ada):
    m, d = c_all.shape
    n = w_ada.shape[1]
    tn = 512
    return pl.pallas_call(
        _ada_kernel,
        grid=(n // tn,),
        in_specs=[pl.BlockSpec((m, d), lambda j: (0, 0)),
                  pl.BlockSpec((d, tn), lambda j: (0, j)),
                  pl.BlockSpec((1, tn), lambda j: (0, j))],
        out_specs=pl.BlockSpec((m, tn), lambda j: (0, j)),
        out_shape=jax.ShapeDtypeStruct((m, n), F32),
        compiler_params=_params("arbitrary"),
        name="ada_mod",
    )(c_all, w_ada, b_ada)


def _inproj_kernel(*refs, sample, tm, d_conv, q_lora, kv_lora):
    it = iter(refs)
    x_ref, sh_ref, sc_ref, g_ref = next(it), next(it), next(it), next(it)
    win = next(it)
    cw_ref = next(it)
    if sample:
        st0_ref, st1_ref = next(it), next(it)
    qg_ref, kvg_ref = next(it), next(it)
    wuq = next(it)
    wukv = None if sample else next(it)
    cos_ref, sin_ref = next(it), next(it)
    if sample:
        yconv_ref, qn_ref, qr_ref, ckv_ref, kr_ref, cn0_ref, cn1_ref = it
    else:
        yconv_ref, q_ref, k_ref, v_ref, ckv_ref, kr_ref, cst_ref, carry_ref = it

    def proj(a, lo, hi_col):
        return _mm(a, win[:, lo:hi_col])

    x = x_ref[...]
    h = _rms(x, g_ref[...]) * (1.0 + _mod_row(sc_ref, sample)) + _mod_row(sh_ref, sample)

    c1, c2, c3 = d_conv, 2 * d_conv, 3 * d_conv
    u = proj(h, c2, c3) * proj(h, 0, c1)
    if sample:
        u1, u2 = st1_ref[...], st0_ref[...]
        cn0_ref[...] = u1
        cn1_ref[...] = u
    else:
        @pl.when(pl.program_id(0) == 0)
        def _():
            carry_ref[...] = jnp.zeros_like(carry_ref)
        prev = carry_ref[...]
        p1, p2 = prev[SUBLANE - 1:SUBLANE, :], prev[SUBLANE - 2:SUBLANE - 1, :]
        rows = lax.broadcasted_iota(I32, u.shape, 0)
        u1 = jnp.where(rows == 0, p1, pltpu.roll(u, 1, 0))
        u2 = jnp.where(rows == 0, p2, jnp.where(rows == 1, p1, pltpu.roll(u, 2, 0)))
        tail = u[tm - SUBLANE:tm, :]
        carry_ref[...] = tail
        cst_ref[...] = tail
    conv = cw_ref[0:1, :] * u2 + cw_ref[1:2, :] * u1 + cw_ref[2:3, :] * u
    yconv_ref[...] = (proj(h, c1, c2) * conv).astype(yconv_ref.dtype)

    cos, sin = cos_ref[...], sin_ref[...]
    c4, c5 = c3 + q_lora, c3 + q_lora + kv_lora
    cqn = _rms(proj(h, c3, c4), qg_ref[...])
    ckv = _rms(proj(h, c4, c5), kvg_ref[...])
    kr_raw = proj(h, c5, c5 + QK_ROPE)
    kr = _rope(jnp.concatenate([kr_raw, jnp.zeros((tm, LANE - QK_ROPE), F32)], axis=1), cos, sin)
    ckv_ref[...] = ckv
    kr_ref[...] = kr[:, :QK_ROPE]
    hq = N_HEADS * QK_NOPE
    q = _mm(cqn, wuq[...])
    if sample:
        qn_ref[...] = q[:, :hq]
        qr_ref[...] = _rope(q[:, hq:], cos, sin)
    else:
        q = q * ATTN_SCALE
        qn = q[:, :hq]
        qr = _rope(q[:, hq:], cos, sin)
        kv = _mm(ckv, wukv[...])
        krb = kr.astype(BF16)
        for hd in range(N_HEADS):
            lo, hi = hd * LANE, (hd + 1) * LANE
            q_ref[hd, :, 0:QK_NOPE] = qn[:, lo:hi].astype(BF16)
            q_ref[hd, :, QK_NOPE:DQK] = qr[:, lo:hi].astype(BF16)
            k_ref[hd, :, 0:QK_NOPE] = kv[:, lo:hi].astype(BF16)
            k_ref[hd, :, QK_NOPE:DQK] = krb
            v_ref[hd] = kv[:, hq + lo:hq + hi].astype(BF16)


def _inproj(x, mod, mod_row_block, norm_g, win, conv_w, state, q_g, kv_g, wuq, wukv, cos, sin, *,
            sample, tm):
    n, d = x.shape
    d_conv = conv_w.shape[1]
    q_lora, kv_lora = q_g.shape[1], kv_g.shape[1]
    rows = tm if sample else SUBLANE
    hq = N_HEADS * QK_NOPE

    def tile(width):
        return pl.BlockSpec((tm, width), lambda i: (i, 0))

    def modspec(col):
        return pl.BlockSpec((rows, d), lambda i: (mod_row_block, col))

    args = [x, mod, mod, norm_g, win, conv_w]
    specs = [tile(d), modspec(0), modspec(1), _resident(norm_g.shape), _resident(win.shape),
             _resident(conv_w.shape)]
    if sample:
        args += list(state)
        specs += [tile(d_conv), tile(d_conv)]
    args += [q_g, kv_g, wuq]
    specs += [_resident(q_g.shape), _resident(kv_g.shape), _resident(wuq.shape)]
    if not sample:
        args.append(wukv)
        specs.append(_resident(wukv.shape))
    args += [cos, sin]
    specs += [tile(LANE), tile(LANE)]

    if sample:
        out_shape = [jax.ShapeDtypeStruct((n, d_conv), F32), jax.ShapeDtypeStruct((n, hq), F32),
                     jax.ShapeDtypeStruct((n, N_HEADS * ROPE_PAD), F32),
                     jax.ShapeDtypeStruct((n, kv_lora), F32), jax.ShapeDtypeStruct((n, QK_ROPE), F32),
                     jax.ShapeDtypeStruct((n, d_conv), F32), jax.ShapeDtypeStruct((n, d_conv), F32)]
        out_specs = [tile(d_conv), tile(hq), tile(N_HEADS * ROPE_PAD), tile(kv_lora), tile(QK_ROPE),
                     tile(d_conv), tile(d_conv)]
        scratch = []
    else:
        def heads(width):
            return pl.BlockSpec((N_HEADS, tm, width), lambda i: (0, i, 0))
        out_shape = [jax.ShapeDtypeStruct((n, d_conv), BF16),
                     jax.ShapeDtypeStruct((N_HEADS, n, DQK), BF16),
                     jax.ShapeDtypeStruct((N_HEADS, n, DQK), BF16),
                     jax.ShapeDtypeStruct((N_HEADS, n, V_HEAD), BF16),
                     jax.ShapeDtypeStruct((n, kv_lora), F32), jax.ShapeDtypeStruct((n, QK_ROPE), F32),
                     jax.ShapeDtypeStruct((SUBLANE, d_conv), F32)]
        out_specs = [tile(d_conv), heads(DQK), heads(DQK), heads(V_HEAD), tile(kv_lora), tile(QK_ROPE),
                     pl.BlockSpec((SUBLANE, d_conv), lambda i: (0, 0))]
        scratch = [pltpu.VMEM((SUBLANE, d_conv), F32)]

    return pl.pallas_call(
        functools.partial(_inproj_kernel, sample=sample, tm=tm, d_conv=d_conv, q_lora=q_lora,
                          kv_lora=kv_lora),
        grid=(n // tm,),
        in_specs=specs, out_specs=out_specs, out_shape=out_shape, scratch_shapes=scratch,
        compiler_params=_params("arbitrary"),
        name="inproj_sample" if sample else "inproj_prompt",
    )(*args)


def _qlat_kernel(qn_ref, w_ref, o_ref):
    o_ref[...] = _dot_t(qn_ref[...].astype(BF16), w_ref[:, :QK_NOPE].astype(BF16))


def _qlat(qn, w_ukv):
    n = qn.shape[0]
    kv_lora = w_ukv.shape[0]
    per_head = QK_NOPE + V_HEAD
    return pl.pallas_call(
        _qlat_kernel,
        grid=(N_HEADS,),
        in_specs=[pl.BlockSpec((n, QK_NOPE), lambda h: (0, h)),
                  pl.BlockSpec((kv_lora, per_head), lambda h: (0, h))],
        out_specs=pl.BlockSpec((n, kv_lora), lambda h: (0, h)),
        out_shape=jax.ShapeDtypeStruct((n, N_HEADS * kv_lora), F32),
        compiler_params=_params("arbitrary"),
        name="sample_qlat",
    )(qn, w_ukv)


def _uv_kernel(o_ref_in, w_ref, y_ref):
    y_ref[...] = _mm(o_ref_in[...], w_ref[:, QK_NOPE:])


def _uv(o_lat, w_ukv):
    n = o_lat.shape[0]
    kv_lora = w_ukv.shape[0]
    per_head = QK_NOPE + V_HEAD
    return pl.pallas_call(
        _uv_kernel,
        grid=(N_HEADS,),
        in_specs=[pl.BlockSpec((n, kv_lora), lambda h: (0, h)),
                  pl.BlockSpec((kv_lora, per_head), lambda h: (0, h))],
        out_specs=pl.BlockSpec((n, V_HEAD), lambda h: (0, h)),
        out_shape=jax.ShapeDtypeStruct((n, N_HEADS * V_HEAD), F32),
        compiler_params=_params("arbitrary"),
        name="sample_uv",
    )(o_lat, w_ukv)


def _attn_kernel(pt_ref, hp_ref, qi_ref, ki_ref,
                 q_ref, k_ref, v_ref, ql_ref, qr_ref, cn_ref, kn_ref, ckv_hbm, kr_hbm,
                 y_ref, o_ref,
                 fm_sc, fl_sc, facc_sc, bias_sc, ckv_buf, kr_buf, sem, dm_sc, dl_sc, dacc_sc,
                 *, tq, heads, pages, page, n_seq, nj):
    del hp_ref
    g = pl.program_id(0)
    n_dec = n_seq * nj
    qi, ki = qi_ref[g], ki_ref[g]
    reps = tq // LANE

    @pl.when(g == 0)
    def _():
        row = lax.broadcasted_iota(I32, (tq, tq), 0)
        col = lax.broadcasted_iota(I32, (tq, tq), 1)
        bias_sc[0] = jnp.zeros((tq, tq), F32)
        bias_sc[1] = jnp.where(col <= row, 0.0, -jnp.inf)

    @pl.when(ki == 0)
    def _():
        fm_sc[...] = jnp.full_like(fm_sc, -jnp.inf)
        fl_sc[...] = jnp.zeros_like(fl_sc)
        facc_sc[...] = jnp.zeros_like(facc_sc)

    def prompt_pieces():
        off = pl.multiple_of(ki * tq, tq)
        st = {}

        def scores(hd):
            def emit():
                st[hd] = (_dot_t(q_ref[hd], k_ref[hd, pl.ds(off, tq), :])
                          + bias_sc[(ki == qi).astype(I32)])
            return emit

        def softmax(hd):
            def emit():
                s = st.pop(hd)
                m_prev = fm_sc[hd]
                m_new = jnp.maximum(m_prev, jnp.max(s, axis=1, keepdims=True))
                alpha = jnp.exp(m_prev - m_new)
                p = jnp.exp(s - jnp.concatenate([m_new] * reps, axis=1))
                fl_sc[hd] = alpha * fl_sc[hd] + jnp.sum(p, axis=1, keepdims=True)
                fm_sc[hd] = m_new
                st[hd] = (alpha, p.astype(BF16))
            return emit

        def values(hd):
            def emit():
                alpha, p = st.pop(hd)
                facc_sc[hd] = alpha * facc_sc[hd] + _dot(p, v_ref[hd, pl.ds(off, tq), :])
            return emit

        return [f(hd) for hd in range(heads) for f in (scores, softmax, values)]

    def chunk_copies(seq, chunk, slot):
        out = []
        for p in range(pages):
            pid = pt_ref[seq, chunk * pages + p]
            rows = pl.ds(p * page, page)
            out.append(pltpu.make_async_copy(ckv_hbm.at[pid], ckv_buf.at[slot, rows], sem.at[slot, 0]))
            out.append(pltpu.make_async_copy(kr_hbm.at[pid], kr_buf.at[slot, :, rows], sem.at[slot, 1]))
        return out

    def sample_pieces(j, slot):
        first = j == 0
        sub = pages * page // SAMPLE_SPLIT
        per = 1
        st = {}

        def rows(k):
            return slice(k * sub, (k + 1) * sub)

        def scores(k):
            def emit():
                if k == 0:
                    st["ql"] = ql_ref[0].astype(BF16).astype(F32)
                    st["qr"] = qr_ref[0].astype(BF16).astype(F32)
                st["s", k] = (_dot_t(st["ql"], ckv_buf[slot, rows(k), :])
                              + _dot(st["qr"], kr_buf[slot, :, rows(k)])) * ATTN_SCALE
            return emit

        def softmax(u):
            def emit():
                s = jnp.concatenate([st.pop(("s", k)) for k in range(u * per, (u + 1) * per)], axis=1)
                if u == 0:
                    m_prev = jnp.where(first, -jnp.inf, dm_sc[...])
                    l_prev = jnp.where(first, 0.0, dl_sc[...])
                    acc_prev = jnp.where(first, 0.0, dacc_sc[...])
                else:
                    m_prev, l_prev, acc_prev = st["m"], st["l"], st["acc"]
                m_new = jnp.maximum(m_prev, jnp.max(s, axis=1, keepdims=True))
                alpha = jnp.exp(m_prev - m_new)
                st["pr", u] = jnp.exp(s - m_new)
                st["m"] = m_new
                st["l"] = alpha * l_prev + jnp.sum(st["pr", u], axis=1, keepdims=True)
                st["acc"] = alpha * acc_prev
            return emit

        def values(k):
            def emit():
                u, off = k // per, (k % per) * sub
                st["acc"] = st["acc"] + _dot(st["pr", u][:, off:off + sub], ckv_buf[slot, rows(k), :])
                if k == SAMPLE_SPLIT - 1:
                    dm_sc[...] = st["m"]
                    dl_sc[...] = st["l"]
                    dacc_sc[...] = st["acc"]
            return emit

        assert SAMPLE_SPLIT == 4
        return [scores(0), softmax(0), scores(1), values(0), softmax(1), scores(2), values(1), softmax(2),
                scores(3), values(2), softmax(3), values(3)]

    @pl.when(g < n_dec)
    def _():
        s_idx, j = g // nj, g % nj

        @pl.when(g == 0)
        def _():
            for c in chunk_copies(0, 0, 0):
                c.start()

        @pl.when(g + 1 < n_dec)
        def _():
            for c in chunk_copies((g + 1) // nj, (g + 1) % nj, (g + 1) % 2):
                c.start()

        slot = g % 2
        for c in chunk_copies(s_idx, j, slot):
            c.wait()
        a, b = sample_pieces(j, slot), prompt_pieces()
        assert heads == 2 and SAMPLE_SPLIT == 4
        order = ([b[0], b[3], a[0], b[1], a[1], a[2], b[2], a[3], a[4], a[5], b[4], a[6], a[7], a[8], b[5]]
                 + a[9:])
        for piece in order:
            piece()

        @pl.when(j == nj - 1)
        def _():
            ql = ql_ref[0].astype(BF16).astype(F32)
            qr = qr_ref[0].astype(BF16).astype(F32)
            cn = cn_ref[0].astype(BF16).astype(F32)
            kn = kn_ref[0].astype(BF16).astype(F32)
            s_new = (jnp.sum(ql * cn, axis=1, keepdims=True)
                     + jnp.sum(qr * kn, axis=1, keepdims=True)) * ATTN_SCALE
            m_old = dm_sc[...]
            m_fin = jnp.maximum(m_old, s_new)
            a_fin = jnp.exp(m_old - m_fin)
            p_new = jnp.exp(s_new - m_fin)
            l_fin = a_fin * dl_sc[...] + p_new
            o_ref[0] = (a_fin * dacc_sc[...] + p_new * cn) / l_fin

    @pl.when(g >= n_dec)
    def _():
        for piece in prompt_pieces():
            piece()

    @pl.when(ki == qi)
    def _():
        for hd in range(heads):
            y_ref[:, hd * V_HEAD:(hd + 1) * V_HEAD] = (facc_sc[hd] / fl_sc[hd]).astype(y_ref.dtype)


def _attn(q, k, v, page_table, q_lat, q_rope, ckv_new, kr_new, cache_ckv, cache_kr_t, *, tq, heads,
          pages):
    h, t, _ = q.shape
    n_seq, n_pages = page_table.shape
    hq, kv_lora = q_lat.shape[1:]
    rope = q_rope.shape[2]
    page = cache_ckv.shape[1]
    assert V_HEAD == LANE and h % heads == 0 and t % tq == 0 and n_pages % pages == 0
    nj = n_pages // pages
    n_dec = n_seq * nj
    nq = t // tq
    trips = [(hp, a, b) for hp in range(h // heads) for a in range(nq) for b in range(a + 1)]
    assert len(trips) >= n_dec, "the sample chunks ride on the prompt's attention trips"
    hp_tab, qi_tab, ki_tab = (jnp.asarray([tr[i] for tr in trips], I32) for i in range(3))

    def seq_block(width_shape):
        return pl.BlockSpec(width_shape, lambda g, *_: (jnp.minimum(g // nj, n_seq - 1), 0, 0))

    grid_spec = pltpu.PrefetchScalarGridSpec(
        num_scalar_prefetch=4,
        grid=(len(trips),),
        in_specs=[pl.BlockSpec((heads, tq, DQK), lambda g, pt, hp, qi, ki: (hp[g], qi[g], 0)),
                  pl.BlockSpec((heads, t, DQK), lambda g, pt, hp, qi, ki: (hp[g], 0, 0),
                               pipeline_mode=pl.Buffered(1)),
                  pl.BlockSpec((heads, t, V_HEAD), lambda g, pt, hp, qi, ki: (hp[g], 0, 0),
                               pipeline_mode=pl.Buffered(1)),
                  seq_block((1, hq, kv_lora)), seq_block((1, hq, rope)),
                  seq_block((1, 1, kv_lora)), seq_block((1, 1, rope)),
                  pl.BlockSpec(memory_space=pl.ANY), pl.BlockSpec(memory_space=pl.ANY)],
        out_specs=[pl.BlockSpec((tq, heads * V_HEAD), lambda g, pt, hp, qi, ki: (qi[g], hp[g])),
                   seq_block((1, hq, kv_lora))],
        scratch_shapes=[pltpu.VMEM((heads, tq, LANE), F32), pltpu.VMEM((heads, tq, LANE), F32),
                        pltpu.VMEM((heads, tq, V_HEAD), F32),
                        pltpu.VMEM((2, tq, tq), F32),
                        pltpu.VMEM((2, pages * page, kv_lora), F32),
                        pltpu.VMEM((2, rope, pages * page), F32),
                        pltpu.SemaphoreType.DMA((2, 2)),
                        pltpu.VMEM((hq, 1), F32), pltpu.VMEM((hq, 1), F32),
                        pltpu.VMEM((hq, kv_lora), F32)])
    return pl.pallas_call(
        functools.partial(_attn_kernel, tq=tq, heads=heads, pages=pages, page=page, n_seq=n_seq, nj=nj),
        grid_spec=grid_spec,
        out_shape=[jax.ShapeDtypeStruct((t, h * V_HEAD), BF16),
                   jax.ShapeDtypeStruct((n_seq, hq, kv_lora), F32)],
        compiler_params=_params("arbitrary"),
        name="attention",
    )(page_table, hp_tab, qi_tab, ki_tab, q, k, v, q_lat, q_rope, ckv_new, kr_new, cache_ckv,
      cache_kr_t)


def _oproj_kernel(*refs, sample, tm, d_conv, n_experts, n_groups):
    it = iter(refs)
    x_ref, yc_ref, ya_ref, g1_ref, sh_ref, sc_ref, ng_ref = (next(it) for _ in range(7))
    wo, router, cnt_in_ref = next(it), next(it), next(it)
    x1_ref, h2_ref, info_ref, cnt_ref, carry_ref = it

    d_mix = wo.shape[0]
    o = _mm(yc_ref[...], wo[0:d_conv, :]) + _mm(ya_ref[...], wo[d_conv:d_mix, :])
    x1 = x_ref[...] + _mod_row(g1_ref, sample) * o
    x1_ref[...] = x1
    h2 = _rms(x1, ng_ref[...]) * (1.0 + _mod_row(sc_ref, sample)) + _mod_row(sh_ref, sample)
    h2_ref[...] = h2

    lg = _mm(h2, router[...])
    lane_i = lax.broadcasted_iota(I32, lg.shape, 1)
    lane = lane_i.astype(F32)
    big = float(LANE)
    neg = -jnp.inf
    gmask = (lane_i >= n_experts) & (lane_i < n_experts + n_groups)
    gl = jnp.where(gmask, lg, neg)
    gmax = jnp.max(gl, axis=1, keepdims=True)
    g_sel = jnp.min(jnp.where(gl == gmax, lane, big), axis=1, keepdims=True) - float(n_experts)
    g_w = 1.0 / jnp.sum(jnp.where(gmask, jnp.exp(lg - gmax), 0.0), axis=1, keepdims=True)
    per_group = n_experts // n_groups
    emask = (lane_i < n_experts) & ((lane_i // per_group).astype(F32) == g_sel)
    el = jnp.where(emask, lg, neg)
    v1 = jnp.max(el, axis=1, keepdims=True)
    i1 = jnp.min(jnp.where(el == v1, lane, big), axis=1, keepdims=True)
    el2 = jnp.where(lane == i1, neg, el)
    v2 = jnp.max(el2, axis=1, keepdims=True)
    i2 = jnp.min(jnp.where(el2 == v2, lane, big), axis=1, keepdims=True)
    e2 = jnp.exp(v2 - v1)
    w1 = (1.0 / (1.0 + e2)) * g_w
    w2 = (e2 / (1.0 + e2)) * g_w

    @pl.when(pl.program_id(0) == 0)
    def _():
        carry_ref[...] = cnt_in_ref[...]
    hit1, hit2 = lane == i1, lane == i2
    chosen = jnp.where(hit1 | hit2, 1.0, 0.0)
    row = lax.broadcasted_iota(I32, (tm, tm), 0)
    col = lax.broadcasted_iota(I32, (tm, tm), 1)
    before = jnp.where(col < row, 1.0, 0.0).astype(BF16)
    running = _dot(before, chosen.astype(BF16)) + carry_ref[0:1, :]
    rank1 = jnp.sum(jnp.where(hit1, running, 0.0), axis=1, keepdims=True)
    rank2 = jnp.sum(jnp.where(hit2, running, 0.0), axis=1, keepdims=True)
    total = carry_ref[...] + jnp.sum(chosen, axis=0, keepdims=True)
    carry_ref[...] = total
    cnt_ref[...] = total

    info = jnp.where(lane_i == 0, i1, 0.0)
    info = jnp.where(lane_i == 1, i2, info)
    info = jnp.where(lane_i == 2, rank1, info)
    info = jnp.where(lane_i == 3, rank2, info)
    info = jnp.where(lane_i == 4, w1, info)
    info = jnp.where(lane_i == 5, w2, info)
    info_ref[...] = info


def _oproj(x, yconv, yattn, mod, mod_row_block, norm_g, wo, router, cnt_in, *, sample, tm,
           n_experts, n_groups):
    n, d = x.shape
    d_conv = yconv.shape[1]
    rows = tm if sample else SUBLANE

    def tile(width):
        return pl.BlockSpec((tm, width), lambda i: (i, 0))

    def modspec(col):
        return pl.BlockSpec((rows, d), lambda i: (mod_row_block, col))

    args = [x, yconv, yattn, mod, mod, mod, norm_g]
    specs = [tile(d), tile(d_conv), tile(yattn.shape[1]), modspec(2), modspec(3), modspec(4),
             _resident(norm_g.shape)]
    for w in (wo, router, cnt_in):
        args.append(w)
        specs.append(_resident(w.shape))
    return pl.pallas_call(
        functools.partial(_oproj_kernel, sample=sample, tm=tm, d_conv=d_conv, n_experts=n_experts,
                          n_groups=n_groups),
        grid=(n // tm,),
        in_specs=specs,
        out_specs=[tile(d), tile(d), tile(LANE), pl.BlockSpec((SUBLANE, LANE), lambda i: (0, 0))],
        out_shape=[jax.ShapeDtypeStruct((n, d), F32), jax.ShapeDtypeStruct((n, d), F32),
                   jax.ShapeDtypeStruct((n, LANE), F32), jax.ShapeDtypeStruct((SUBLANE, LANE), F32)],
        scratch_shapes=[pltpu.VMEM((SUBLANE, LANE), F32)],
        compiler_params=_params("arbitrary"),
        name="oproj_sample" if sample else "oproj_prompt",
    )(*args)


def _row_copy(src, src_row, dst, dst_row, sem):
    return pltpu.make_async_copy(src.at[pl.ds(src_row, 1)], dst.at[pl.ds(dst_row, 1)], sem)


def _scatter_kernel(dest_ref, fill_ref, hp_ref, hs_ref, xs_ref, zero_ref, sem, *, tm, n_prompt_tiles):
    i = pl.program_id(0)

    def scatter_rows(h_ref, n_rows, base):
        def start(r, carry):
            t = base + r
            _row_copy(h_ref, r, xs_ref, dest_ref[2 * t], sem).start()
            _row_copy(h_ref, r, xs_ref, dest_ref[2 * t + 1], sem).start()
            return carry

        lax.fori_loop(0, n_rows, start, 0, unroll=8)
        all_rows = xs_ref.at[pl.ds(0, 2 * n_rows)]
        pltpu.make_async_copy(all_rows, all_rows, sem).wait()

    @pl.when(i < n_prompt_tiles)
    def _():
        scatter_rows(hp_ref, tm, i * tm)

    @pl.when(i == n_prompt_tiles)
    def _():
        scatter_rows(hs_ref, hs_ref.shape[0], n_prompt_tiles * tm)

    @pl.when(i == 0)
    def _():
        zero_ref[...] = jnp.zeros_like(zero_ref)

        def each_range(act):
            def per_range(e, carry):
                first, count = fill_ref[2 * e], fill_ref[2 * e + 1]
                lax.fori_loop(0, count, lambda r, c: act(first + r, c), 0)
                return carry
            lax.fori_loop(0, fill_ref.shape[0] // 2, per_range, 0)

        def zstart(slot, c):
            _row_copy(zero_ref, 0, xs_ref, slot, sem).start()
            return c

        def zwait(slot, c):
            _row_copy(zero_ref, 0, xs_ref, 0, sem).wait()
            return c

        each_range(zstart)
        each_range(zwait)


def _scatter(dest, fill, h2_p, h2_s, n_slots, *, tm):
    n_p, d = h2_p.shape
    n_prompt_tiles = n_p // tm
    grid_spec = pltpu.PrefetchScalarGridSpec(
        num_scalar_prefetch=2,
        grid=(n_prompt_tiles + 1,),
        in_specs=[pl.BlockSpec((tm, d), lambda i, *_: (jnp.minimum(i, n_prompt_tiles - 1), 0)),
                  pl.BlockSpec(h2_s.shape, lambda i, *_: (0, 0))],
        out_specs=pl.BlockSpec(memory_space=pl.ANY),
        scratch_shapes=[pltpu.VMEM((SUBLANE, d), F32), pltpu.SemaphoreType.DMA(())])
    return pl.pallas_call(
        functools.partial(_scatter_kernel, tm=tm, n_prompt_tiles=n_prompt_tiles),
        grid_spec=grid_spec,
        out_shape=jax.ShapeDtypeStruct((n_slots, d), F32),
        compiler_params=_params("arbitrary"),
        name="moe_scatter",
    )(dest, fill, h2_p, h2_s)


def _ffn_kernel(blk_e_ref, nused_ref, wfirst_ref, wslot_ref, wnext_ref, xs_ref, wg_hbm, wu_hbm, wd_hbm,
                o_ref, wg_buf, wu_buf, wd_buf, sem):
    b = pl.program_id(0)

    def weight_copies(e, slot):
        return [pltpu.make_async_copy(wg_hbm.at[e], wg_buf.at[slot], sem.at[slot, 0]),
                pltpu.make_async_copy(wu_hbm.at[e], wu_buf.at[slot], sem.at[slot, 1]),
                pltpu.make_async_copy(wd_hbm.at[e], wd_buf.at[slot], sem.at[slot, 2])]

    @pl.when(b == 0)
    def _():
        for c in weight_copies(blk_e_ref[0], 0):
            c.start()

    slot = wslot_ref[b]

    @pl.when(wfirst_ref[b] == 1)
    def _():
        for c in weight_copies(blk_e_ref[b], slot):
            c.wait()

        @pl.when(wnext_ref[b] >= 0)
        def _():
            for c in weight_copies(wnext_ref[b], 1 - slot):
                c.start()

    used = b < nused_ref[0]

    @pl.when(used)
    def _():
        x = xs_ref[...]
        hdn = jax.nn.silu(_dot(x, wg_buf[slot])) * _dot(x, wu_buf[slot])
        o_ref[...] = _dot(hdn, wd_buf[slot])

    @pl.when(jnp.logical_not(used))
    def _():
        o_ref[...] = jnp.zeros_like(o_ref)


def _ffn(blk_e, nused, xs, w_gate, w_up, w_down):
    n_slots, d = xs.shape
    n_blocks = n_slots // ROUTE_BLOCK
    d_e = w_gate.shape[2]

    blocks = jnp.arange(n_blocks, dtype=I32)
    first = (blocks < nused[0]) & ((blocks == 0) | (blk_e != jnp.roll(blk_e, 1)))
    wslot = (jnp.cumsum(first.astype(I32)) - 1) % 2
    first_pos = jnp.where(first, blocks, n_blocks)
    at_or_after = lax.cummin(first_pos[::-1])[::-1]
    next_pos = jnp.concatenate([at_or_after[1:], jnp.full((1,), n_blocks, I32)])
    wnext = jnp.sum(jnp.where(blocks[None, :] == next_pos[:, None], blk_e[None, :] + 1, 0), axis=1) - 1

    def rows(b, be, nu, *_):
        return (jnp.minimum(b, nu[0] - 1), 0)

    grid_spec = pltpu.PrefetchScalarGridSpec(
        num_scalar_prefetch=5,
        grid=(n_blocks,),
        in_specs=[pl.BlockSpec((ROUTE_BLOCK, d), rows),
                  pl.BlockSpec(memory_space=pl.ANY), pl.BlockSpec(memory_space=pl.ANY),
                  pl.BlockSpec(memory_space=pl.ANY)],
        out_specs=pl.BlockSpec((ROUTE_BLOCK, d), lambda b, *_: (b, 0)),
        scratch_shapes=[pltpu.VMEM((2, d, d_e), F32), pltpu.VMEM((2, d, d_e), F32),
                        pltpu.VMEM((2, d_e, d), F32), pltpu.SemaphoreType.DMA((2, 3))])
    return pl.pallas_call(
        _ffn_kernel,
        grid_spec=grid_spec,
        out_shape=jax.ShapeDtypeStruct((n_slots, d), F32),
        compiler_params=_params("arbitrary"),
        name="moe_experts",
    )(blk_e, nused, first.astype(I32), wslot.astype(I32), wnext.astype(I32), xs, w_gate, w_up, w_down)


def _final_kernel(dest_ref, x1_ref, info_ref, g2_ref, fg_ref, yb_ref, o_ref, rows_ref, sem, *,
                  tm, base, per_row):
    i = pl.program_id(0)
    n = pl.num_programs(0)

    def issue(tile_idx, slot):
        def start(r, carry):
            t = base + tile_idx * tm + r
            _row_copy(yb_ref, dest_ref[2 * t], rows_ref.at[slot], r, sem.at[slot]).start()
            _row_copy(yb_ref, dest_ref[2 * t + 1], rows_ref.at[slot], tm + r, sem.at[slot]).start()
            return carry
        lax.fori_loop(0, tm, start, 0, unroll=8)

    @pl.when(i == 0)
    def _():
        issue(0, 0)

    @pl.when(i + 1 < n)
    def _():
        issue(i + 1, (i + 1) % 2)

    slot = i % 2

    pltpu.make_async_copy(yb_ref.at[pl.ds(0, 2 * tm)], rows_ref.at[slot], sem.at[slot]).wait()
    info = info_ref[...]
    moe = info[:, 4:5] * rows_ref[slot, 0:tm] + info[:, 5:6] * rows_ref[slot, tm:2 * tm]
    y = x1_ref[...] + _mod_row(g2_ref, per_row) * moe
    o_ref[...] = _rms(y, fg_ref[...])


def _final(dest, x1, info, mod, mod_row_block, final_g, yb, *, tm, base, per_row):
    n, d = x1.shape
    rows = tm if per_row else SUBLANE
    grid_spec = pltpu.PrefetchScalarGridSpec(
        num_scalar_prefetch=1,
        grid=(n // tm,),
        in_specs=[pl.BlockSpec((tm, d), lambda i, *_: (i, 0)),
                  pl.BlockSpec((tm, LANE), lambda i, *_: (i, 0)),
                  pl.BlockSpec((rows, d), lambda i, *_: (mod_row_block, N_MOD - 1)),
                  pl.BlockSpec((1, d), lambda i, *_: (0, 0)),
                  pl.BlockSpec(memory_space=pl.ANY)],
        out_specs=pl.BlockSpec((tm, d), lambda i, *_: (i, 0)),
        scratch_shapes=[pltpu.VMEM((2, 2 * tm, d), F32), pltpu.SemaphoreType.DMA((2,))])
    return pl.pallas_call(
        functools.partial(_final_kernel, tm=tm, base=base, per_row=per_row),
        grid_spec=grid_spec,
        out_shape=jax.ShapeDtypeStruct((n, d), F32),
        compiler_params=_params("arbitrary"),
        name="moe_combine_final",
    )(dest, x1, info, mod, final_g, yb)


def _rope_tables(pos):
    half = QK_ROPE // 2
    inv_freq = ROPE_THETA ** (-jnp.arange(half, dtype=F32) / half)
    ang = pos.astype(F32)[:, None] * inv_freq
    c, s = jnp.cos(ang), jnp.sin(ang)
    z = jnp.zeros((pos.shape[0], LANE - QK_ROPE), F32)
    return jnp.concatenate([c, c, z], axis=1), jnp.concatenate([-s, s, z], axis=1)


def kernel(x_prompt, x_sample, cache_ckv, cache_krope, state_conv, page_table, c_prompt, c_sample,
           w_ada, b_ada, norm_mix_g, norm_ffn_g, w_in, conv_w, q_norm_g, w_uq, kv_norm_g, w_ukv, w_o,
           router_group, router_expert, w_gate, w_up, w_down, final_g):
    depth = w_ada.shape[0]
    bp, tp, d = x_prompt.shape
    bs, ts, _ = x_sample.shape
    assert depth == 1 and bp == 1 and ts == 1, "one layer, one prompt sequence, one new token per sample"
    n_pages = page_table.shape[1]
    page = cache_ckv.shape[2]
    d_conv = conv_w.shape[2]
    q_lora, kv_lora = q_norm_g.shape[1], kv_norm_g.shape[1]
    n_groups = router_group.shape[2]
    n_experts = router_expert.shape[2]
    assert n_experts + n_groups <= LANE and bs % SUBLANE == 0
    tm_p = 256
    assert tp % tm_p == 0 and (2 * tp) % ROUTE_BLOCK == 0

    c_all = jnp.concatenate([c_sample, c_prompt, jnp.zeros((SUBLANE - bp, d), F32)], axis=0)
    mod = _ada(c_all, w_ada[0], b_ada[0][None, :])
    prompt_mod_block = bs // SUBLANE

    win = w_in[0].astype(BF16)
    wuq3 = w_uq[0].astype(BF16).reshape(q_lora, N_HEADS, QK_NOPE + QK_ROPE)
    wuq = jnp.concatenate(
        [wuq3[:, :, :QK_NOPE].reshape(q_lora, N_HEADS * QK_NOPE),
         jnp.pad(wuq3[:, :, QK_NOPE:], ((0, 0), (0, 0), (0, ROPE_PAD - QK_ROPE))).reshape(
             q_lora, N_HEADS * ROPE_PAD)], axis=1)
    wukv3 = w_ukv[0].astype(BF16).reshape(kv_lora, N_HEADS, QK_NOPE + V_HEAD)
    wukv = jnp.concatenate(
        [wukv3[:, :, :QK_NOPE].reshape(kv_lora, N_HEADS * QK_NOPE),
         wukv3[:, :, QK_NOPE:].reshape(kv_lora, N_HEADS * V_HEAD)], axis=1)
    wo = w_o[0].astype(BF16)
    router = jnp.pad(jnp.concatenate([router_expert[0], router_group[0]], axis=1).astype(BF16),
                     ((0, 0), (0, LANE - n_experts - n_groups)))
    g_mix, g_ffn = norm_mix_g[0][None, :], norm_ffn_g[0][None, :]
    q_g, kv_g = q_norm_g[0][None, :], kv_norm_g[0][None, :]

    cos_p, sin_p = _rope_tables(jnp.arange(tp, dtype=I32))
    xp = x_prompt.reshape(tp, d)
    yconv_p, q_p, k_p, v_p, ckv_p, kr_p, cst_p = _inproj(
        xp, mod, prompt_mod_block, g_mix, win, conv_w[0], None, q_g, kv_g, wuq, wukv, cos_p, sin_p,
        sample=False, tm=tm_p)

    past = n_pages * page
    cos_s, sin_s = _rope_tables(jnp.full((bs,), past, I32))
    xs_tok = x_sample.reshape(bs, d)
    yconv_s, qn_s, qr_s, ckv_s, kr_s, cn0_s, cn1_s = _inproj(
        xs_tok, mod, 0, g_mix, win, conv_w[0], (state_conv[0, :, 0], state_conv[0, :, 1]), q_g,
        kv_g, wuq, None, cos_s, sin_s, sample=True, tm=bs)
    q_lat = _qlat(qn_s, w_ukv[0]).reshape(bs, N_HEADS, kv_lora)
    q_rope = qr_s.reshape(bs, N_HEADS, ROPE_PAD)[:, :, :QK_ROPE]
    cache_kr_t = jnp.swapaxes(cache_krope[0], 1, 2)
    yattn_p, o_lat = _attn(q_p, k_p, v_p, page_table, q_lat, q_rope, ckv_s[:, None, :],
                           kr_s[:, None, :], cache_ckv[0], cache_kr_t, tq=512, heads=2, pages=32)
    yattn_s = _uv(o_lat.reshape(bs, N_HEADS * kv_lora), w_ukv[0])

    zeros_cnt = jnp.zeros((SUBLANE, LANE), F32)
    x1_p, h2_p, info_p, cnt_p = _oproj(
        xp, yconv_p, yattn_p, mod, prompt_mod_block, g_ffn, wo, router, zeros_cnt,
        sample=False, tm=tm_p, n_experts=n_experts, n_groups=n_groups)
    x1_s, h2_s, info_s, cnt_s = _oproj(
        xs_tok, yconv_s, yattn_s, mod, 0, g_ffn, wo, router, cnt_p,
        sample=True, tm=bs, n_experts=n_experts, n_groups=n_groups)

    n_tok = tp + bs
    info = jnp.concatenate([info_p[:, :4], info_s[:, :4]], axis=0).astype(I32)
    counts = cnt_s[0, :n_experts].astype(I32)
    padded = (counts + ROUTE_BLOCK - 1) // ROUTE_BLOCK * ROUTE_BLOCK
    pend = jnp.cumsum(padded)
    pstart = pend - padded
    experts = jnp.arange(n_experts, dtype=I32)
    first_slot = jnp.sum(jnp.where(info[:, 0:2, None] == experts, pstart, 0), axis=-1)
    dest = (first_slot + info[:, 2:4]).reshape(-1)
    n_blocks = -(-(n_tok * 2) // ROUTE_BLOCK) + n_experts
    n_slots = n_blocks * ROUTE_BLOCK
    nused = (pend[-1] // ROUTE_BLOCK).astype(I32)
    blk = jnp.minimum(jnp.arange(n_blocks, dtype=I32), nused - 1) * ROUTE_BLOCK
    blk_e = jnp.minimum(jnp.sum((pend[None, :] <= blk[:, None]).astype(I32), axis=1), n_experts - 1)
    fill = jnp.stack([jnp.append(pstart + counts, pend[-1]),
                      jnp.append(padded - counts, n_slots - pend[-1])], axis=1).reshape(-1).astype(I32)

    xs = _scatter(dest, fill, h2_p, h2_s, n_slots, tm=tm_p)
    yb = _ffn(blk_e, nused.reshape(1), xs, w_gate[0], w_up[0], w_down[0])

    fg = final_g[None, :]
    y_p = _final(dest, x1_p, info_p, mod, prompt_mod_block, fg, yb, tm=tm_p, base=0, per_row=False)
    y_s = _final(dest, x1_s, info_s, mod, 0, fg, yb, tm=bs, base=tp, per_row=True)

    conv_p = cst_p[SUBLANE - 2:, :]
    conv_s = jnp.stack([cn0_s, cn1_s], axis=1)
    return (y_p.reshape(bp, tp, d), y_s.reshape(bs, ts, d),
            ckv_p.reshape(depth, bp, tp, kv_lora), kr_p.reshape(depth, bp, tp, QK_ROPE),
            conv_p.reshape(depth, bp, 2, d_conv),
            ckv_s.reshape(depth, bs, ts, kv_lora), kr_s.reshape(depth, bs, ts, QK_ROPE),
            conv_s.reshape(depth, bs, 2, d_conv))
```
